```python
import math
import jax, jax.numpy as jnp
from jax import lax
import numpy as np

D_MODEL = 1024
BATCH = 8
SEQ = 4096
DEPTH = 1

HGRN_EXPAND = 128
HGRN_HEADS = D_MODEL // HGRN_EXPAND
HGRN_DK = HGRN_EXPAND
HGRN_DV = D_MODEL // HGRN_HEADS
HGRN_K_WIDTH = HGRN_HEADS * HGRN_DK
HGRN_V_WIDTH = HGRN_HEADS * HGRN_DV
CHUNK = 64
CONV_CH = D_MODEL
CONV_K = 31
D_FF = 2816
FFN_RESIDUAL = 0.5
N_MOD = 9
EPS = 1e-6
IN_SIZES = (HGRN_K_WIDTH, HGRN_K_WIDTH, HGRN_V_WIDTH, HGRN_V_WIDTH, 2 * CONV_CH, D_MODEL, D_MODEL)
IN_WIDTH = sum(IN_SIZES)
IN_SPLITS = tuple(int(s) for s in np.cumsum(IN_SIZES)[:-1])

kernel_name = "hybrid_hgrn2_conformer_macaron_adaln"


def rms_norm(x, g):
    xf = x.astype(jnp.float32)
    y = xf * lax.rsqrt(jnp.mean(xf * xf, axis=-1, keepdims=True) + EPS)
    return (y * g.astype(jnp.float32)).astype(x.dtype)


def layer_norm(x, g, b):
    xf = x.astype(jnp.float32)
    mu = jnp.mean(xf, axis=-1, keepdims=True)
    xc = xf - mu
    y = xc * lax.rsqrt(jnp.mean(xc * xc, axis=-1, keepdims=True) + EPS)
    return (y * g.astype(jnp.float32) + b.astype(jnp.float32)).astype(x.dtype)


def swiglu(h, w_in, w_out):
    a, b = jnp.split(h @ w_in, 2, axis=-1)
    return (jax.nn.silu(a) * b) @ w_out


def hgrn2_chunked(q, k, v, logf):
    B, S, H, DK = q.shape
    DV = v.shape[-1]
    nc = S // CHUNK

    def to_chunks(t):
        return t.reshape(B, nc, CHUNK, H, t.shape[-1]).transpose(1, 0, 3, 2, 4)

    causal = jnp.tril(jnp.ones((CHUNK, CHUNK), dtype=bool))

    def step(state, inp):
        qc, kc, vc, gc = inp
        b = jnp.cumsum(gc, axis=-2)
        diff = b[:, :, :, None, :] - b[:, :, None, :, :]
        decay = jnp.exp(jnp.where(causal[:, :, None], diff, -jnp.inf))
        att = jnp.einsum('bhtd,bhsd,bhtsd->bhts', qc, kc, decay)
        o_intra = jnp.einsum('bhts,bhsv->bhtv', att, vc)
        o_inter = jnp.einsum('bhtd,bhdv->bhtv', qc * jnp.exp(b), state)
        b_last = b[:, :, -1, :]
        k_dec = kc * jnp.exp(b_last[:, :, None, :] - b)
        new_state = jnp.exp(b_last)[..., None] * state + jnp.einsum('bhsd,bhsv->bhdv', k_dec, vc)
        return new_state, o_intra + o_inter

    state0 = jnp.zeros((B, H, DK, DV), jnp.float32)
    _, o = lax.scan(step, state0, (to_chunks(q), to_chunks(k), to_chunks(v), to_chunks(logf)))
    return o.transpose(1, 0, 3, 2, 4).reshape(B, S, H, DV)


def causal_depthwise_conv(u, w, b):
    y = lax.conv_general_dilated(
        u, w[:, None, :].astype(u.dtype), window_strides=(1,), padding=[(CONV_K - 1, 0)],
        dimension_numbers=('NWC', 'WIO', 'NWC'), feature_group_count=u.shape[-1])
    return y + b


def token_mixer(h, lb, w_in, hgrn_g, hgrn_w_o, conv_w, conv_b, conv_ln_g, conv_ln_b, conv_w_o, w_out):
    B, S, _ = h.shape
    f32 = jnp.float32
    q, f, i, og, u, ga, gb = jnp.split(h @ w_in, IN_SPLITS, axis=-1)
    q = (jax.nn.silu(q.astype(f32)) * (HGRN_DK ** -0.5)).reshape(B, S, HGRN_HEADS, HGRN_DK)
    fg = lb + (1.0 - lb) * jax.nn.sigmoid(f.astype(f32))
    logf = jnp.log(fg).reshape(B, S, HGRN_HEADS, HGRN_DK)
    k = (1.0 - fg).reshape(B, S, HGRN_HEADS, HGRN_DK)
    v = i.astype(f32).reshape(B, S, HGRN_HEADS, HGRN_DV)
    o = hgrn2_chunked(q, k, v, logf)
    o = o * lax.rsqrt(jnp.mean(o * o, axis=-1, keepdims=True) + EPS)
    o = o * hgrn_g.astype(f32).reshape(HGRN_HEADS, HGRN_DV)
    o = (o.reshape(B, S, HGRN_V_WIDTH) * jax.nn.silu(og.astype(f32))).astype(h.dtype)
    y_a = o @ hgrn_w_o
    ua, ub = jnp.split(u, 2, axis=-1)
    u = ua * jax.nn.sigmoid(ub)
    u = causal_depthwise_conv(u, conv_w, conv_b)
    u = jax.nn.silu(layer_norm(u, conv_ln_g, conv_ln_b))
    y_b = u @ conv_w_o
    merged = jax.nn.sigmoid(ga) * y_a + jax.nn.sigmoid(gb) * y_b
    return merged @ w_out


def setup_inputs(seed: int = 0) -> dict:
    key = jax.random.key(seed)
    ks = jax.random.split(key, 24)
    D, L = D_MODEL, DEPTH
    nrm = lambda k, shape, fan_in: jax.random.normal(k, shape, jnp.float32) * (fan_in ** -0.5)
    gain = lambda k, shape: 1.0 + 0.02 * jax.random.normal(k, shape, jnp.float32)
    small = lambda k, shape: 0.02 * jax.random.normal(k, shape, jnp.float32)
    return {
        "x": jax.random.normal(ks[0], (BATCH, SEQ, D), jnp.float32),
        "c": jax.random.normal(ks[1], (BATCH, D), jnp.float32),
        "ada_w": nrm(ks[2], (L, D, N_MOD * D), D),
        "ada_b": small(ks[3], (L, N_MOD * D)),
        "norm_ffn1": gain(ks[4], (L, D)),
        "ffn1_w_in": nrm(ks[5], (L, D, 2 * D_FF), D),
        "ffn1_w_out": nrm(ks[6], (L, D_FF, D), D_FF),
        "norm_mix": gain(ks[7], (L, D)),
        "mix_w_in": nrm(ks[8], (L, D, IN_WIDTH), D),
        "hgrn_lb": 0.1 * jax.random.normal(ks[9], (L + 1, HGRN_K_WIDTH), jnp.float32),
        "hgrn_g": gain(ks[10], (L, HGRN_V_WIDTH)),
        "hgrn_w_o": nrm(ks[11], (L, HGRN_V_WIDTH, D), HGRN_V_WIDTH),
        "conv_w": nrm(ks[12], (L, CONV_K, CONV_CH), CONV_K),
        "conv_b": small(ks[13], (L, CONV_CH)),
        "conv_ln_g": gain(ks[14], (L, CONV_CH)),
        "conv_ln_b": small(ks[15], (L, CONV_CH)),
        "conv_w_o": nrm(ks[16], (L, CONV_CH, D), CONV_CH),
        "mix_w_out": nrm(ks[17], (L, D, D), D),
        "norm_ffn2": gain(ks[18], (L, D)),
        "ffn2_w_in": nrm(ks[19], (L, D, 2 * D_FF), D),
        "ffn2_w_out": nrm(ks[20], (L, D_FF, D), D_FF),
        "norm_final": gain(ks[21], (D,)),
    }


def reference(x, c, ada_w, ada_b, norm_ffn1, ffn1_w_in, ffn1_w_out, norm_mix, mix_w_in,
              hgrn_lb, hgrn_g, hgrn_w_o, conv_w, conv_b, conv_ln_g, conv_ln_b, conv_w_o,
              mix_w_out, norm_ffn2, ffn2_w_in, ffn2_w_out, norm_final):
    B = x.shape[0]
    lb_all = jnp.cumsum(jax.nn.softmax(hgrn_lb.astype(jnp.float32), axis=0), axis=0)
    cs = jax.nn.silu(c)
    for l in range(DEPTH):
        mod = (cs @ ada_w[l] + ada_b[l]).reshape(B, N_MOD, D_MODEL)
        sh1, sc1, g1, sh2, sc2, g2, sh3, sc3, g3 = [mod[:, j, None, :] for j in range(N_MOD)]
        h = rms_norm(x, norm_ffn1[l]) * (1.0 + sc1) + sh1
        x = x + FFN_RESIDUAL * g1 * swiglu(h, ffn1_w_in[l], ffn1_w_out[l])
        h = rms_norm(x, norm_mix[l]) * (1.0 + sc2) + sh2
        x = x + g2 * token_mixer(h, lb_all[l], mix_w_in[l], hgrn_g[l], hgrn_w_o[l], conv_w[l],
                                 conv_b[l], conv_ln_g[l], conv_ln_b[l], conv_w_o[l], mix_w_out[l])
        h = rms_norm(x, norm_ffn2[l]) * (1.0 + sc3) + sh3
        x = x + FFN_RESIDUAL * g3 * swiglu(h, ffn2_w_in[l], ffn2_w_out[l])
    return rms_norm(x, norm_final)
```

```python
import functools

import jax
import jax.numpy as jnp
from jax import lax
from jax.experimental import pallas as pl
from jax.experimental.pallas import tpu as pltpu

EPS = 1e-6
N_MOD = 9
HGRN_HEAD_DIM = 128
CONV_K = 31
FFN_RESIDUAL = 0.5

V7X_VMEM_BYTES = 64 * 1024 * 1024
V7X_MXU_COL = 256
SUBLANES = 8

FFN_TOKENS = 512
MIX_TOKENS = 256
HGRN_CHUNK = 64
CONV_HALO = 32
CONV_ROWS = 32
FACTORED_DECAY_LIMIT = 60.0

_BF16 = jnp.bfloat16
_F32 = jnp.float32


def _vmem_limit(nbytes):
    return int(min(nbytes * 1.25 + (4 << 20), V7X_VMEM_BYTES - (8 << 20)))


def _resident(shape):
    nd = len(shape)
    return pl.BlockSpec(shape, lambda *_: (0,) * nd, pipeline_mode=pl.Buffered(1))


def _dot(a, b):
    return jnp.dot(a, b, preferred_element_type=_F32)


def _dot_nt(a, b):
    return lax.dot_general(a, b, (((1,), (1,)), ((), ())), preferred_element_type=_F32)


def _dot_tn(a, b):
    return lax.dot_general(a, b, (((0,), (0,)), ((), ())), preferred_element_type=_F32)


def _silu(x):
    return x * jax.nn.sigmoid(x)


def _modulated_rms_norm(x, gain, scale, shift):
    y = x * lax.rsqrt(jnp.mean(x * x, axis=-1, keepdims=True) + EPS)
    return (y * gain) * (1.0 + scale) + shift


def _adaln_kernel(c_ref, w_ref, b_ref, o_ref):
    cs = _silu(c_ref[...]).astype(_BF16)
    o_ref[...] = _dot(cs, w_ref[...].astype(_BF16)) + b_ref[...]


def _adaln_mod(c, ada_w, ada_b):
    B, D = c.shape
    n = ada_w.shape[1]
    return pl.pallas_call(
        _adaln_kernel,
        out_shape=jax.ShapeDtypeStruct((B, n), _F32),
        grid=(n // D,),
        in_specs=[pl.BlockSpec((B, D), lambda j: (0, 0)),
                  pl.BlockSpec((D, D), lambda j: (0, j)),
                  pl.BlockSpec((1, D), lambda j: (0, j))],
        out_specs=pl.BlockSpec((B, D), lambda j: (0, j)),
        compiler_params=pltpu.CompilerParams(dimension_semantics=("arbitrary",)),
        name="adaln_mod",
    )(c, ada_w, ada_b)


def _ffn_kernel(x_ref, mod_ref, gain_ref, w_in_ref, w_out_ref, fin_ref, o_ref, act_ref, *, mod_row, final_norm):
    x = x_ref[...]
    shift = mod_ref[pl.ds(mod_row, 1), :]
    scale = mod_ref[pl.ds(mod_row + 1, 1), :]
    gate = mod_ref[pl.ds(mod_row + 2, 1), :]
    h = _modulated_rms_norm(x, gain_ref[...], scale, shift).astype(_BF16)
    n_chunks = w_in_ref.shape[0]
    fc = w_in_ref.shape[2] // 2
    for j in range(n_chunks):
        ab = _dot(h, w_in_ref[j])
        a, b = ab[:, :fc], ab[:, fc:]
        act_ref[:, j * fc:(j + 1) * fc] = (_silu(a) * b).astype(_BF16)
    y = x + (FFN_RESIDUAL * gate) * _dot(act_ref[...], w_out_ref[...])
    if final_norm:
        y = y * lax.rsqrt(jnp.mean(y * y, axis=-1, keepdims=True) + EPS) * fin_ref[...]
    o_ref[...] = y


def _ffn(x, mod, gain, w_in, w_out, fin_gain, *, mod_row, final_norm):
    B, S, D = x.shape
    F = w_out.shape[0]
    tm = FFN_TOKENS
    vmem = (w_in.size + w_out.size) * 2 + 4 * tm * D * 4 + tm * F * 2 + 3 * tm * D * 4 + 2 * tm * w_in.shape[2] * 4
    return pl.pallas_call(
        functools.partial(_ffn_kernel, mod_row=mod_row, final_norm=final_norm),
        out_shape=jax.ShapeDtypeStruct((B, S, D), _F32),
        grid=(B, S // tm),
        in_specs=[pl.BlockSpec((None, tm, D), lambda b, i: (b, i, 0)),
                  pl.BlockSpec((None, N_MOD, D), lambda b, i: (b, 0, 0)),
                  _resident((1, D)),
                  _resident(w_in.shape),
                  _resident(w_out.shape),
                  _resident((1, D))],
        out_specs=pl.BlockSpec((None, tm, D), lambda b, i: (b, i, 0)),
        scratch_shapes=[pltpu.VMEM((tm, F), _BF16)],
        compiler_params=pltpu.CompilerParams(dimension_semantics=("arbitrary", "arbitrary"),
                                             vmem_limit_bytes=_vmem_limit(vmem)),
        name="ffn_final" if final_norm else "ffn",
    )(x, mod, gain, w_in, w_out, fin_gain)


def _mix_in_kernel(x_ref, mod_ref, gain_ref, w_ref, lb_ref,
                   q_ref, g_ref, k_ref, v_ref, og_ref, u_ref, ga_ref, gb_ref):
    D = x_ref.shape[-1]
    shift = mod_ref[pl.ds(3, 1), :]
    scale = mod_ref[pl.ds(4, 1), :]
    h = _modulated_rms_norm(x_ref[...], gain_ref[...], scale, shift).astype(_BF16)

    def proj(i):
        return _dot(h, w_ref[:, i * D:(i + 1) * D])

    lb_all = lb_ref[...]
    e = jnp.exp(lb_all - jnp.max(lb_all, axis=0, keepdims=True))
    lb = e[0:1, :] / jnp.sum(e, axis=0, keepdims=True)

    q_ref[...] = _silu(proj(0)) * (HGRN_HEAD_DIM ** -0.5)
    fg = lb + (1.0 - lb) * jax.nn.sigmoid(proj(1))
    g_ref[...] = jnp.log(fg)
    k_ref[...] = 1.0 - fg
    v_ref[...] = proj(2).astype(_BF16)
    og_ref[...] = _silu(proj(3)).astype(_BF16)
    u_ref[...] = proj(4) * jax.nn.sigmoid(proj(5))
    ga_ref[...] = jax.nn.sigmoid(proj(6)).astype(_BF16)
    gb_ref[...] = jax.nn.sigmoid(proj(7)).astype(_BF16)


def _mix_in(x, mod, gain, w_in, hgrn_lb):
    B, S, D = x.shape
    tm = MIX_TOKENS
    tile = pl.BlockSpec((None, tm, D), lambda b, i: (b, i, 0))
    out_dtypes = (_F32, _F32, _F32, _BF16, _BF16, _F32, _BF16, _BF16)
    vmem = w_in.size * 2 + 2 * tm * D * 4 + sum(2 * tm * D * jnp.dtype(t).itemsize for t in out_dtypes) + 6 * tm * D * 4
    return pl.pallas_call(
        _mix_in_kernel,
        out_shape=[jax.ShapeDtypeStruct((B, S, D), t) for t in out_dtypes],
        grid=(B, S // tm),
        in_specs=[tile,
                  pl.BlockSpec((None, N_MOD, D), lambda b, i: (b, 0, 0)),
                  _resident((1, D)),
                  _resident(w_in.shape),
                  _resident(hgrn_lb.shape)],
        out_specs=[tile] * len(out_dtypes),
        compiler_params=pltpu.CompilerParams(dimension_semantics=("arbitrary", "arbitrary"),
                                             vmem_limit_bytes=_vmem_limit(vmem)),
        name="mix_in",
    )(x, mod, gain, w_in, hgrn_lb)


def _hgrn_kernel(q_ref, g_ref, k_ref, v_ref, og_ref, gain_ref, o_ref, state_ref, cum_ref, intra_ref):
    ts, width = q_ref.shape
    hd = HGRN_HEAD_DIM
    n_heads = width // hd
    C = HGRN_CHUNK
    n_chunks = ts // C

    @pl.when(pl.program_id(1) == 0)
    def _():
        state_ref[...] = jnp.zeros_like(state_ref)

    row = lax.broadcasted_iota(jnp.int32, (ts, ts), 0)
    col = lax.broadcasted_iota(jnp.int32, (ts, ts), 1)
    tri = jnp.where(row >= col, 1.0, 0.0).astype(_BF16)
    g = g_ref[...]
    g_hi = g.astype(_BF16)
    g_lo = (g - g_hi.astype(_F32)).astype(_BF16)
    cum_ref[...] = _dot(tri, g_hi) + _dot(tri, g_lo)

    def chunk_decay(c, lanes):
        b = cum_ref[c * C:(c + 1) * C, lanes]
        if c > 0:
            b = b - cum_ref[c * C - 1:c * C, lanes]
        return b, b[C - 1:C, :]

    total_decay = cum_ref[C - 1:C, :]
    for c in range(1, n_chunks):
        total_decay = jnp.minimum(total_decay, cum_ref[(c + 1) * C - 1:(c + 1) * C, :] - cum_ref[c * C - 1:c * C, :])
    factored_ok = jnp.min(total_decay) >= -FACTORED_DECAY_LIMIT

    causal = lax.broadcasted_iota(jnp.int32, (C, C), 0) >= lax.broadcasted_iota(jnp.int32, (C, C), 1)

    def run(exact_intra):
        for h in range(n_heads):
            lanes = slice(h * hd, (h + 1) * hd)
            state = state_ref[h]
            outs = []
            for c in range(n_chunks):
                rows = slice(c * C, (c + 1) * C)
                b, b_last = chunk_decay(c, lanes)
                q = q_ref[rows, lanes]
                k = k_ref[rows, lanes]
                v = v_ref[rows, lanes]
                q_dec = (q * jnp.exp(b)).astype(_BF16)
                o = _dot_nt(q_dec, state.astype(_BF16))
                if exact_intra:
                    o = o + intra_ref[rows, lanes]
                else:
                    k_inc = (k * jnp.exp(-b)).astype(_BF16)
                    att = jnp.where(causal, _dot_nt(q_dec, k_inc), 0.0).astype(_BF16)
                    o = o + _dot(att, v)
                k_dec = (k * jnp.exp(b_last - b)).astype(_BF16)
                state = state * jnp.exp(b_last) + _dot_tn(v, k_dec)
                outs.append(o)
            state_ref[h] = state
            o = jnp.concatenate(outs, axis=0)
            o = o * lax.rsqrt(jnp.mean(o * o, axis=-1, keepdims=True) + EPS)
            o = o * gain_ref[:, lanes] * og_ref[:, lanes].astype(_F32)
            o_ref[:, lanes] = o.astype(o_ref.dtype)

    @pl.when(factored_ok)
    def _():
        run(exact_intra=False)

    @pl.when(jnp.logical_not(factored_ok))
    def _():
        s_idx = lax.broadcasted_iota(jnp.int32, (C, 1), 0)

        def row_group(r, carry):
            base = pl.multiple_of(r * SUBLANES, SUBLANES)
            chunk_start = pl.multiple_of((base // C) * C, C)
            for h in range(n_heads):
                lanes = slice(h * hd, (h + 1) * hd)
                q8 = q_ref[pl.ds(base, SUBLANES), lanes]
                b8 = cum_ref[pl.ds(base, SUBLANES), lanes]
                k_c = k_ref[pl.ds(chunk_start, C), lanes]
                b_c = cum_ref[pl.ds(chunk_start, C), lanes]
                v_c = v_ref[pl.ds(chunk_start, C), lanes].astype(_F32)
                out_rows = []
                for i in range(SUBLANES):
                    decay = jnp.exp(jnp.minimum(b8[i:i + 1, :] - b_c, 0.0))
                    w = jnp.where(s_idx <= base - chunk_start + i, q8[i:i + 1, :] * k_c * decay, 0.0)
                    att = jnp.sum(w, axis=-1, keepdims=True)
                    out_rows.append(jnp.sum(att * v_c, axis=0, keepdims=True))
                intra_ref[pl.ds(base, SUBLANES), lanes] = jnp.concatenate(out_rows, axis=0)
            return carry

        lax.fori_loop(0, ts // SUBLANES, row_group, 0)
        run(exact_intra=True)


def _hgrn(q, g, k, v, og, gain):
    B, S, W = q.shape
    ts = MIX_TOKENS
    n_heads = W // HGRN_HEAD_DIM
    tile = pl.BlockSpec((None, ts, W), lambda b, i: (b, i, 0))
    vmem = 2 * ts * W * (3 * 4 + 3 * 2) + n_heads * HGRN_HEAD_DIM ** 2 * 4 + 2 * ts * W * 4 + 8 * ts * W * 4
    return pl.pallas_call(
        _hgrn_kernel,
        out_shape=jax.ShapeDtypeStruct((B, S, W), _BF16),
        grid=(B, S // ts),
        in_specs=[tile, tile, tile, tile, tile, _resident((1, W))],
        out_specs=tile,
        scratch_shapes=[pltpu.VMEM((n_heads, HGRN_HEAD_DIM, HGRN_HEAD_DIM), _F32),
                        pltpu.VMEM((ts, W), _F32),
                        pltpu.VMEM((ts, W), _F32)],
        compiler_params=pltpu.CompilerParams(dimension_semantics=("arbitrary", "arbitrary"),
                                             vmem_limit_bytes=_vmem_limit(vmem)),
        name="hgrn",
    )(q, g, k, v, og, gain)


def _mix_out_kernel(x_ref, mod_ref, u_ref, ya_ref, ga_ref, gb_ref,
                    conv_w_ref, conv_b_ref, ln_g_ref, ln_b_ref, wa_ref, wb_ref, wo_ref,
                    o_ref, ubuf_ref, conv_ref):
    tm = x_ref.shape[0]
    halo = CONV_HALO

    @pl.when(pl.program_id(1) == 0)
    def _():
        ubuf_ref[0:halo, :] = jnp.zeros((halo, ubuf_ref.shape[1]), _F32)

    ubuf_ref[halo:, :] = u_ref[...]
    first = halo - (CONV_K - 1)
    for r in range(0, tm, CONV_ROWS):
        acc = jnp.broadcast_to(conv_b_ref[...], (CONV_ROWS, conv_b_ref.shape[1]))
        for j in range(CONV_K):
            acc = acc + conv_w_ref[j:j + 1, :] * ubuf_ref[r + first + j:r + first + j + CONV_ROWS, :]
        conv_ref[r:r + CONV_ROWS, :] = acc
    ubuf_ref[0:halo, :] = ubuf_ref[tm:tm + halo, :]

    y = conv_ref[...]
    mu = jnp.mean(y, axis=-1, keepdims=True)
    yc = y - mu
    yn = yc * lax.rsqrt(jnp.mean(yc * yc, axis=-1, keepdims=True) + EPS)
    z = _silu(yn * ln_g_ref[...] + ln_b_ref[...]).astype(_BF16)

    y_a = _dot(ya_ref[...], wa_ref[...])
    y_b = _dot(z, wb_ref[...])
    merged = (ga_ref[...].astype(_F32) * y_a + gb_ref[...].astype(_F32) * y_b).astype(_BF16)
    gate = mod_ref[pl.ds(5, 1), :]
    o_ref[...] = x_ref[...] + gate * _dot(merged, wo_ref[...])


def _mix_out(x, mod, u, ya, ga, gb, conv_w, conv_b, ln_g, ln_b, wa, wb, wo):
    B, S, D = x.shape
    tm = MIX_TOKENS
    tile = pl.BlockSpec((None, tm, D), lambda b, i: (b, i, 0))
    vmem = 3 * D * D * 2 + 2 * tm * D * (4 + 4 + 2 + 2 + 2 + 4) + (2 * tm + CONV_HALO) * D * 4 + 8 * tm * D * 4
    return pl.pallas_call(
        _mix_out_kernel,
        out_shape=jax.ShapeDtypeStruct((B, S, D), _F32),
        grid=(B, S // tm),
        in_specs=[tile,
                  pl.BlockSpec((None, N_MOD, D), lambda b, i: (b, 0, 0)),
                  tile, tile, tile, tile,
                  _resident(conv_w.shape), _resident((1, D)), _resident((1, D)), _resident((1, D)),
                  _resident(wa.shape), _resident(wb.shape), _resident(wo.shape)],
        out_specs=tile,
        scratch_shapes=[pltpu.VMEM((CONV_HALO + tm, D), _F32),
                        pltpu.VMEM((tm, D), _F32)],
        compiler_params=pltpu.CompilerParams(dimension_semantics=("arbitrary", "arbitrary"),
                                             vmem_limit_bytes=_vmem_limit(vmem)),
        name="mix_out",
    )(x, mod, u, ya, ga, gb, conv_w, conv_b, ln_g, ln_b, wa, wb, wo)


def _ffn_weights(w_in, w_out):
    D, two_f = w_in.shape
    F = two_f // 2
    fc = V7X_MXU_COL
    a = w_in[:, :F].reshape(D, F // fc, fc)
    b = w_in[:, F:].reshape(D, F // fc, fc)
    w = jnp.concatenate([a, b], axis=-1).transpose(1, 0, 2)
    return w.astype(_BF16), w_out.astype(_BF16)


def kernel(x, c, ada_w, ada_b, norm_ffn1, ffn1_w_in, ffn1_w_out, norm_mix, mix_w_in, hgrn_lb, hgrn_g, hgrn_w_o,
           conv_w, conv_b, conv_ln_g, conv_ln_b, conv_w_o, mix_w_out, norm_ffn2, ffn2_w_in, ffn2_w_out, norm_final):
    B, S, D = x.shape
    depth = ada_w.shape[0]
    assert depth == 1, "single-layer block"
    assert S % FFN_TOKENS == 0 and S % MIX_TOKENS == 0 and MIX_TOKENS % HGRN_CHUNK == 0
    assert mix_w_in.shape[-1] == 8 * D and D % HGRN_HEAD_DIM == 0
    assert conv_w.shape[1] == CONV_K and CONV_K - 1 <= CONV_HALO

    mod = _adaln_mod(c, ada_w[0], ada_b).reshape(B, N_MOD, D)
    fin = norm_final.reshape(1, D)

    w_in1, w_out1 = _ffn_weights(ffn1_w_in[0], ffn1_w_out[0])
    x = _ffn(x, mod, norm_ffn1, w_in1, w_out1, fin, mod_row=0, final_norm=False)

    q, g, k, v, og, u, ga, gb = _mix_in(x, mod, norm_mix, mix_w_in[0].astype(_BF16), hgrn_lb)
    ya = _hgrn(q, g, k, v, og, hgrn_g)
    x = _mix_out(x, mod, u, ya, ga, gb, conv_w[0], conv_b, conv_ln_g, conv_ln_b,
                 hgrn_w_o[0].astype(_BF16), conv_w_o[0].astype(_BF16), mix_w_out[0].astype(_BF16))

    w_in2, w_out2 = _ffn_weights(ffn2_w_in[0], ffn2_w_out[0])
    return _ffn(x, mod, norm_ffn2, w_in2, w_out2, fin, mod_row=6, final_norm=True)
```

```python
import functools

import jax
import jax.numpy as jnp
from jax import lax
from jax.experimental import pallas as pl
from jax.experimental.pallas import tpu as pltpu

EPS = 1e-6
N_MOD = 9
HGRN_HEAD_DIM = 128
CONV_K = 31
FFN_RESIDUAL = 0.5

V7X_VMEM_BYTES = 64 * 1024 * 1024
V7X_MXU_COL = 256
SUBLANES = 8

FFN_TOKENS = 512
MIX_TOKENS = 256
HGRN_CHUNK = 128
LANES = 128
CONV_HALO = 32
CONV_ROWS = 64
FACTORED_DECAY_LIMIT = 80.0

_BF16 = jnp.bfloat16
_F32 = jnp.float32


def _vmem_limit(nbytes):
    return int(min(nbytes * 1.25 + (4 << 20), V7X_VMEM_BYTES - (8 << 20)))


def _resident(shape):
    nd = len(shape)
    return pl.BlockSpec(shape, lambda *_: (0,) * nd, pipeline_mode=pl.Buffered(1))


def _dot(a, b):
    return jnp.dot(a, b, preferred_element_type=_F32)


def _dot_nt(a, b):
    return lax.dot_general(a, b, (((1,), (1,)), ((), ())), preferred_element_type=_F32)


def _dot_tn(a, b):
    return lax.dot_general(a, b, (((0,), (0,)), ((), ())), preferred_element_type=_F32)


def _silu(x):
    return x * jax.nn.sigmoid(x)


def _modulated_rms_norm(x, gain, scale, shift):
    y = x * lax.rsqrt(jnp.mean(x * x, axis=-1, keepdims=True) + EPS)
    return (y * gain) * (1.0 + scale) + shift


def _adaln_kernel(c_ref, w_ref, b_ref, o_ref):
    cs = _silu(c_ref[...]).astype(_BF16)
    o_ref[...] = _dot(cs, w_ref[...].astype(_BF16)) + b_ref[...]


def _adaln_mod(c, ada_w, ada_b):
    B, D = c.shape
    n = ada_w.shape[1]
    return pl.pallas_call(
        _adaln_kernel,
        out_shape=jax.ShapeDtypeStruct((B, n), _F32),
        grid=(n // D,),
        in_specs=[pl.BlockSpec((B, D), lambda j: (0, 0)),
                  pl.BlockSpec((D, D), lambda j: (0, j)),
                  pl.BlockSpec((1, D), lambda j: (0, j))],
        out_specs=pl.BlockSpec((B, D), lambda j: (0, j)),
        compiler_params=pltpu.CompilerParams(dimension_semantics=("arbitrary",)),
        name="adaln_mod",
    )(c, ada_w, ada_b)


def _ffn_kernel(x_ref, mod_ref, gain_ref, w_in_ref, w_out_ref, fin_ref, o_ref, act_ref, *, mod_row, final_norm):
    x = x_ref[...]
    shift = mod_ref[pl.ds(mod_row, 1), :]
    scale = mod_ref[pl.ds(mod_row + 1, 1), :]
    gate = mod_ref[pl.ds(mod_row + 2, 1), :]
    h = _modulated_rms_norm(x, gain_ref[...], scale, shift).astype(_BF16)
    n_chunks = w_in_ref.shape[0]
    fc = w_in_ref.shape[2] // 2
    for j in range(n_chunks):
        ab = _dot(h, w_in_ref[j])
        a, b = ab[:, :fc], ab[:, fc:]
        act_ref[:, j * fc:(j + 1) * fc] = (_silu(a) * b).astype(_BF16)
    y = x + (FFN_RESIDUAL * gate) * _dot(act_ref[...], w_out_ref[...])
    if final_norm:
        y = y * lax.rsqrt(jnp.mean(y * y, axis=-1, keepdims=True) + EPS) * fin_ref[...]
    o_ref[...] = y


def _ffn(x, mod, gain, w_in, w_out, fin_gain, *, mod_row, final_norm):
    B, S, D = x.shape
    F = w_out.shape[0]
    tm = FFN_TOKENS
    vmem = (w_in.size + w_out.size) * 2 + 4 * tm * D * 4 + tm * F * 2 + 3 * tm * D * 4 + 2 * tm * w_in.shape[2] * 4
    return pl.pallas_call(
        functools.partial(_ffn_kernel, mod_row=mod_row, final_norm=final_norm),
        out_shape=jax.ShapeDtypeStruct((B, S, D), _F32),
        grid=(B, S // tm),
        in_specs=[pl.BlockSpec((None, tm, D), lambda b, i: (b, i, 0)),
                  pl.BlockSpec((None, N_MOD, D), lambda b, i: (b, 0, 0)),
                  _resident((1, D)),
                  _resident(w_in.shape),
                  _resident(w_out.shape),
                  _resident((1, D))],
        out_specs=pl.BlockSpec((None, tm, D), lambda b, i: (b, i, 0)),
        scratch_shapes=[pltpu.VMEM((tm, F), _BF16)],
        compiler_params=pltpu.CompilerParams(dimension_semantics=("arbitrary", "arbitrary"),
                                             vmem_limit_bytes=_vmem_limit(vmem)),
        name="ffn_final" if final_norm else "ffn",
    )(x, mod, gain, w_in, w_out, fin_gain)


def _mix_in_kernel(x_ref, mod_ref, gain_ref, w_ref, lb_ref,
                   q_ref, g_ref, k_ref, v_ref, og_ref, u_ref, ga_ref, gb_ref):
    D = x_ref.shape[-1]
    shift = mod_ref[pl.ds(3, 1), :]
    scale = mod_ref[pl.ds(4, 1), :]
    h = _modulated_rms_norm(x_ref[...], gain_ref[...], scale, shift).astype(_BF16)

    def proj(i):
        return _dot(h, w_ref[:, i * D:(i + 1) * D])

    lb_all = lb_ref[...]
    e = jnp.exp(lb_all - jnp.max(lb_all, axis=0, keepdims=True))
    lb = e[0:1, :] / jnp.sum(e, axis=0, keepdims=True)

    q_ref[...] = _silu(proj(0)) * (HGRN_HEAD_DIM ** -0.5)
    fg = lb + (1.0 - lb) * jax.nn.sigmoid(proj(1))
    g_ref[...] = jnp.log(fg)
    k_ref[...] = 1.0 - fg
    v_ref[...] = proj(2).astype(_BF16)
    og_ref[...] = _silu(proj(3)).astype(_BF16)
    u_ref[...] = proj(4) * jax.nn.sigmoid(proj(5))
    ga_ref[...] = jax.nn.sigmoid(proj(6)).astype(_BF16)
    gb_ref[...] = jax.nn.sigmoid(proj(7)).astype(_BF16)


def _mix_in(x, mod, gain, w_in, hgrn_lb):
    B, S, D = x.shape
    tm = MIX_TOKENS
    tile = pl.BlockSpec((None, tm, D), lambda b, i: (b, i, 0))
    out_dtypes = (_F32, _F32, _F32, _BF16, _BF16, _F32, _BF16, _BF16)
    vmem = w_in.size * 2 + 2 * tm * D * 4 + sum(2 * tm * D * jnp.dtype(t).itemsize for t in out_dtypes) + 6 * tm * D * 4
    return pl.pallas_call(
        _mix_in_kernel,
        out_shape=[jax.ShapeDtypeStruct((B, S, D), t) for t in out_dtypes],
        grid=(B, S // tm),
        in_specs=[tile,
                  pl.BlockSpec((None, N_MOD, D), lambda b, i: (b, 0, 0)),
                  _resident((1, D)),
                  _resident(w_in.shape),
                  _resident(hgrn_lb.shape)],
        out_specs=[tile] * len(out_dtypes),
        compiler_params=pltpu.CompilerParams(dimension_semantics=("arbitrary", "arbitrary"),
                                             vmem_limit_bytes=_vmem_limit(vmem)),
        name="mix_in",
    )(x, mod, gain, w_in, hgrn_lb)


def _hgrn_kernel(q_ref, g_ref, k_ref, v_ref, og_ref, gain_ref, o_ref,
                 state_ref, b_ref, qd_ref, ki_ref, kd_ref, dec_ref, intra_ref):
    ts, width = q_ref.shape
    hd = HGRN_HEAD_DIM
    n_heads = width // hd
    C = HGRN_CHUNK
    n_chunks = ts // C

    @pl.when(pl.program_id(1) == 0)
    def _():
        state_ref[...] = jnp.zeros_like(state_ref)

    causal = lax.broadcasted_iota(jnp.int32, (C, C), 0) >= lax.broadcasted_iota(jnp.int32, (C, C), 1)
    tri = jnp.where(causal, 1.0, 0.0).astype(_BF16)
    total_decay = None
    for c in range(n_chunks):
        rows = slice(c * C, (c + 1) * C)
        g = g_ref[rows, :]
        g_hi = g.astype(_BF16)
        g_lo = (g - g_hi.astype(_F32)).astype(_BF16)
        b = _dot(tri, g_hi) + _dot(tri, g_lo)
        b_last = b[C - 1:C, :]
        total_decay = b_last if total_decay is None else jnp.minimum(total_decay, b_last)
        q = q_ref[rows, :]
        k = k_ref[rows, :]
        b_ref[rows, :] = b
        qd_ref[rows, :] = (q * jnp.exp(b)).astype(_BF16)
        ki_ref[rows, :] = (k * jnp.exp(-b)).astype(_BF16)
        kd_ref[rows, :] = (k * jnp.exp(b_last - b)).astype(_BF16)
        dec_ref[c:c + 1, :] = jnp.exp(b_last)
    factored_ok = jnp.min(total_decay) >= -FACTORED_DECAY_LIMIT

    def run(exact_intra):
        for h in range(n_heads):
            lanes = slice(h * hd, (h + 1) * hd)
            state = state_ref[h]
            outs = []
            for c in range(n_chunks):
                rows = slice(c * C, (c + 1) * C)
                q_dec = qd_ref[rows, lanes]
                v = v_ref[rows, lanes]
                o = _dot_nt(q_dec, state.astype(_BF16))
                if exact_intra:
                    o = o + intra_ref[rows, lanes]
                else:
                    att = jnp.where(causal, _dot_nt(q_dec, ki_ref[rows, lanes]), 0.0).astype(_BF16)
                    o = o + _dot(att, v)
                state = state * dec_ref[c:c + 1, lanes] + _dot_tn(v, kd_ref[rows, lanes])
                outs.append(o)
            state_ref[h] = state
            o = jnp.concatenate(outs, axis=0)
            o = o * lax.rsqrt(jnp.mean(o * o, axis=-1, keepdims=True) + EPS)
            o = o * gain_ref[:, lanes] * og_ref[:, lanes].astype(_F32)
            o_ref[:, lanes] = o.astype(o_ref.dtype)

    @pl.when(factored_ok)
    def _():
        run(exact_intra=False)

    @pl.when(jnp.logical_not(factored_ok))
    def _():
        s_idx = lax.broadcasted_iota(jnp.int32, (C, 1), 0)

        def row_group(r, carry):
            base = pl.multiple_of(r * SUBLANES, SUBLANES)
            chunk_start = pl.multiple_of((base // C) * C, C)
            for h in range(n_heads):
                lanes = slice(h * hd, (h + 1) * hd)
                q8 = q_ref[pl.ds(base, SUBLANES), lanes]
                b8 = b_ref[pl.ds(base, SUBLANES), lanes]
                k_c = k_ref[pl.ds(chunk_start, C), lanes]
                b_c = b_ref[pl.ds(chunk_start, C), lanes]
                v_c = v_ref[pl.ds(chunk_start, C), lanes].astype(_F32)
                out_rows = []
                for i in range(SUBLANES):
                    decay = jnp.exp(jnp.minimum(b8[i:i + 1, :] - b_c, 0.0))
                    w = jnp.where(s_idx <= base - chunk_start + i, q8[i:i + 1, :] * k_c * decay, 0.0)
                    att = jnp.sum(w, axis=-1, keepdims=True)
                    out_rows.append(jnp.sum(att * v_c, axis=0, keepdims=True))
                intra_ref[pl.ds(base, SUBLANES), lanes] = jnp.concatenate(out_rows, axis=0)
            return carry

        lax.fori_loop(0, ts // SUBLANES, row_group, 0)
        run(exact_intra=True)


def _hgrn(q, g, k, v, og, gain):
    B, S, W = q.shape
    ts = MIX_TOKENS
    n_heads = W // HGRN_HEAD_DIM
    tile = pl.BlockSpec((None, ts, W), lambda b, i: (b, i, 0))
    vmem = 2 * ts * W * (3 * 4 + 3 * 2) + n_heads * HGRN_HEAD_DIM ** 2 * 4 + ts * W * (2 * 4 + 3 * 2) + 6 * ts * W * 4
    return pl.pallas_call(
        _hgrn_kernel,
        out_shape=jax.ShapeDtypeStruct((B, S, W), _BF16),
        grid=(B, S // ts),
        in_specs=[tile, tile, tile, tile, tile, _resident((1, W))],
        out_specs=tile,
        scratch_shapes=[pltpu.VMEM((n_heads, HGRN_HEAD_DIM, HGRN_HEAD_DIM), _F32),
                        pltpu.VMEM((ts, W), _F32),
                        pltpu.VMEM((ts, W), _BF16),
                        pltpu.VMEM((ts, W), _BF16),
                        pltpu.VMEM((ts, W), _BF16),
                        pltpu.VMEM((SUBLANES, W), _F32),
                        pltpu.VMEM((ts, W), _F32)],
        compiler_params=pltpu.CompilerParams(dimension_semantics=("arbitrary", "arbitrary"),
                                             vmem_limit_bytes=_vmem_limit(vmem)),
        name="hgrn",
    )(q, g, k, v, og, gain)


def _mix_out_kernel(x_ref, mod_ref, u_ref, ya_ref, ga_ref, gb_ref,
                    conv_w_ref, conv_b_ref, ln_g_ref, ln_b_ref, wa_ref, wb_ref, wo_ref,
                    o_ref, ubuf_ref, conv_ref):
    tm, D = x_ref.shape
    halo = CONV_HALO
    n_lane_tiles = D // LANES

    @pl.when(pl.program_id(1) == 0)
    def _():
        ubuf_ref[:, 0:halo, :] = jnp.zeros((n_lane_tiles, halo, LANES), _F32)

    first = halo - (CONV_K - 1)
    for l in range(n_lane_tiles):
        lanes = slice(l * LANES, (l + 1) * LANES)
        ubuf_ref[l, halo:, :] = u_ref[:, lanes]
        for r in range(0, tm, CONV_ROWS):
            acc = jnp.broadcast_to(conv_b_ref[:, lanes], (CONV_ROWS, LANES))
            for j in range(CONV_K):
                acc = acc + conv_w_ref[j:j + 1, lanes] * ubuf_ref[l, r + first + j:r + first + j + CONV_ROWS, :]
            conv_ref[r:r + CONV_ROWS, lanes] = acc
        ubuf_ref[l, 0:halo, :] = ubuf_ref[l, tm:tm + halo, :]

    y = conv_ref[...]
    mu = jnp.mean(y, axis=-1, keepdims=True)
    yc = y - mu
    yn = yc * lax.rsqrt(jnp.mean(yc * yc, axis=-1, keepdims=True) + EPS)
    z = _silu(yn * ln_g_ref[...] + ln_b_ref[...]).astype(_BF16)

    y_a = _dot(ya_ref[...], wa_ref[...])
    y_b = _dot(z, wb_ref[...])
    merged = (ga_ref[...].astype(_F32) * y_a + gb_ref[...].astype(_F32) * y_b).astype(_BF16)
    gate = mod_ref[pl.ds(5, 1), :]
    o_ref[...] = x_ref[...] + gate * _dot(merged, wo_ref[...])


def _mix_out(x, mod, u, ya, ga, gb, conv_w, conv_b, ln_g, ln_b, wa, wb, wo):
    B, S, D = x.shape
    tm = MIX_TOKENS
    tile = pl.BlockSpec((None, tm, D), lambda b, i: (b, i, 0))
    vmem = 3 * D * D * 2 + 2 * tm * D * (4 + 4 + 2 + 2 + 2 + 4) + (2 * tm + CONV_HALO) * D * 4 + 8 * tm * D * 4
    return pl.pallas_call(
        _mix_out_kernel,
        out_shape=jax.ShapeDtypeStruct((B, S, D), _F32),
        grid=(B, S // tm),
        in_specs=[tile,
                  pl.BlockSpec((None, N_MOD, D), lambda b, i: (b, 0, 0)),
                  tile, tile, tile, tile,
                  _resident(conv_w.shape), _resident((1, D)), _resident((1, D)), _resident((1, D)),
                  _resident(wa.shape), _resident(wb.shape), _resident(wo.shape)],
        out_specs=tile,
        scratch_shapes=[pltpu.VMEM((D // LANES, CONV_HALO + tm, LANES), _F32),
                        pltpu.VMEM((tm, D), _F32)],
        compiler_params=pltpu.CompilerParams(dimension_semantics=("arbitrary", "arbitrary"),
                                             vmem_limit_bytes=_vmem_limit(vmem)),
        name="mix_out",
    )(x, mod, u, ya, ga, gb, conv_w, conv_b, ln_g, ln_b, wa, wb, wo)


def _ffn_weights(w_in, w_out):
    D, two_f = w_in.shape
    F = two_f // 2
    fc = V7X_MXU_COL
    a = w_in[:, :F].reshape(D, F // fc, fc)
    b = w_in[:, F:].reshape(D, F // fc, fc)
    w = jnp.concatenate([a, b], axis=-1).transpose(1, 0, 2)
    return w.astype(_BF16), w_out.astype(_BF16)


def kernel(x, c, ada_w, ada_b, norm_ffn1, ffn1_w_in, ffn1_w_out, norm_mix, mix_w_in, hgrn_lb, hgrn_g, hgrn_w_o,
           conv_w, conv_b, conv_ln_g, conv_ln_b, conv_w_o, mix_w_out, norm_ffn2, ffn2_w_in, ffn2_w_out, norm_final):
    B, S, D = x.shape
    depth = ada_w.shape[0]
    assert depth == 1, "single-layer block"
    assert S % FFN_TOKENS == 0 and S % MIX_TOKENS == 0 and MIX_TOKENS % HGRN_CHUNK == 0
    assert mix_w_in.shape[-1] == 8 * D and D % HGRN_HEAD_DIM == 0
    assert conv_w.shape[1] == CONV_K and CONV_K - 1 <= CONV_HALO and MIX_TOKENS % CONV_ROWS == 0

    mod = _adaln_mod(c, ada_w[0], ada_b).reshape(B, N_MOD, D)
    fin = norm_final.reshape(1, D)

    w_in1, w_out1 = _ffn_weights(ffn1_w_in[0], ffn1_w_out[0])
    x = _ffn(x, mod, norm_ffn1, w_in1, w_out1, fin, mod_row=0, final_norm=False)

    q, g, k, v, og, u, ga, gb = _mix_in(x, mod, norm_mix, mix_w_in[0].astype(_BF16), hgrn_lb)
    ya = _hgrn(q, g, k, v, og, hgrn_g)
    x = _mix_out(x, mod, u, ya, ga, gb, conv_w[0], conv_b, conv_ln_g, conv_ln_b,
                 hgrn_w_o[0].astype(_BF16), conv_w_o[0].astype(_BF16), mix_w_out[0].astype(_BF16))

    w_in2, w_out2 = _ffn_weights(ffn2_w_in[0], ffn2_w_out[0])
    return _ffn(x, mod, norm_ffn2, w_in2, w_out2, fin, mod_row=6, final_norm=True)
```

```python
import functools

import jax
import jax.numpy as jnp
from jax import lax
from jax.experimental import pallas as pl
from jax.experimental.pallas import tpu as pltpu

EPS = 1e-6
N_MOD = 9
HGRN_HEAD_DIM = 128
CONV_K = 31
FFN_RESIDUAL = 0.5

V7X_VMEM_BYTES = 64 * 1024 * 1024
V7X_MXU_COL = 256
SUBLANES = 8

FFN_TOKENS = 512
MIX_TOKENS = 256
HGRN_CHUNK = 128
LANES = 128
CONV_HALO = 32
CONV_ROWS = 64
FACTORED_DECAY_LIMIT = 80.0

_BF16 = jnp.bfloat16
_F32 = jnp.float32


def _vmem_limit(nbytes):
    return int(min(nbytes * 1.25 + (4 << 20), V7X_VMEM_BYTES - (8 << 20)))


def _resident(shape):
    nd = len(shape)
    return pl.BlockSpec(shape, lambda *_: (0,) * nd, pipeline_mode=pl.Buffered(1))


def _dot(a, b):
    return jnp.dot(a, b, preferred_element_type=_F32)


def _dot_nt(a, b):
    return lax.dot_general(a, b, (((1,), (1,)), ((), ())), preferred_element_type=_F32)


def _dot_tn(a, b):
    return lax.dot_general(a, b, (((0,), (0,)), ((), ())), preferred_element_type=_F32)


def _silu(x):
    return x * jax.nn.sigmoid(x)


def _modulated_rms_norm(x, gain, scale, shift):
    y = x * lax.rsqrt(jnp.mean(x * x, axis=-1, keepdims=True) + EPS)
    return (y * gain) * (1.0 + scale) + shift


def _adaln_kernel(c_ref, w_ref, b_ref, o_ref):
    cs = _silu(c_ref[...]).astype(_BF16)
    o_ref[...] = _dot(cs, w_ref[...].astype(_BF16)) + b_ref[...]


def _adaln_mod(c, ada_w, ada_b):
    B, D = c.shape
    n = ada_w.shape[1]
    return pl.pallas_call(
        _adaln_kernel,
        out_shape=jax.ShapeDtypeStruct((B, n), _F32),
        grid=(n // D,),
        in_specs=[pl.BlockSpec((B, D), lambda j: (0, 0)),
                  pl.BlockSpec((D, D), lambda j: (0, j)),
                  pl.BlockSpec((1, D), lambda j: (0, j))],
        out_specs=pl.BlockSpec((B, D), lambda j: (0, j)),
        compiler_params=pltpu.CompilerParams(dimension_semantics=("arbitrary",)),
        name="adaln_mod",
    )(c, ada_w, ada_b)


def _ffn_kernel(x_ref, mod_ref, gain_ref, w_in_ref, w_out_ref, fin_ref, o_ref, act_ref, *, mod_row, final_norm):
    x = x_ref[...]
    shift = mod_ref[pl.ds(mod_row, 1), :]
    scale = mod_ref[pl.ds(mod_row + 1, 1), :]
    gate = mod_ref[pl.ds(mod_row + 2, 1), :]
    h = _modulated_rms_norm(x, gain_ref[...], scale, shift).astype(_BF16)
    n_chunks = w_in_ref.shape[0]
    fc = w_in_ref.shape[2] // 2
    for j in range(n_chunks):
        ab = _dot(h, w_in_ref[j])
        a, b = ab[:, :fc], ab[:, fc:]
        act_ref[:, j * fc:(j + 1) * fc] = (_silu(a) * b).astype(_BF16)
    y = x + (FFN_RESIDUAL * gate) * _dot(act_ref[...], w_out_ref[...])
    if final_norm:
        y = y * lax.rsqrt(jnp.mean(y * y, axis=-1, keepdims=True) + EPS) * fin_ref[...]
    o_ref[...] = y


def _ffn(x, mod, gain, w_in, w_out, fin_gain, *, mod_row, final_norm):
    B, S, D = x.shape
    F = w_out.shape[0]
    tm = FFN_TOKENS
    vmem = (w_in.size + w_out.size) * 2 + 4 * tm * D * 4 + tm * F * 2 + 3 * tm * D * 4 + 2 * tm * w_in.shape[2] * 4
    return pl.pallas_call(
        functools.partial(_ffn_kernel, mod_row=mod_row, final_norm=final_norm),
        out_shape=jax.ShapeDtypeStruct((B, S, D), _F32),
        grid=(B, S // tm),
        in_specs=[pl.BlockSpec((None, tm, D), lambda b, i: (b, i, 0)),
                  pl.BlockSpec((None, N_MOD, D), lambda b, i: (b, 0, 0)),
                  _resident((1, D)),
                  _resident(w_in.shape),
                  _resident(w_out.shape),
                  _resident((1, D))],
        out_specs=pl.BlockSpec((None, tm, D), lambda b, i: (b, i, 0)),
        scratch_shapes=[pltpu.VMEM((tm, F), _BF16)],
        compiler_params=pltpu.CompilerParams(dimension_semantics=("arbitrary", "arbitrary"),
                                             vmem_limit_bytes=_vmem_limit(vmem)),
        name="ffn_final" if final_norm else "ffn",
    )(x, mod, gain, w_in, w_out, fin_gain)


def _mix_in_kernel(x_ref, mod_ref, gain_ref, w_ref, lb_ref,
                   q_ref, g_ref, k_ref, v_ref, og_ref, u_ref, ga_ref, gb_ref):
    D = x_ref.shape[-1]
    shift = mod_ref[pl.ds(3, 1), :]
    scale = mod_ref[pl.ds(4, 1), :]
    h = _modulated_rms_norm(x_ref[...], gain_ref[...], scale, shift).astype(_BF16)

    def proj(i):
        return _dot(h, w_ref[:, i * D:(i + 1) * D])

    lb_all = lb_ref[...]
    e = jnp.exp(lb_all - jnp.max(lb_all, axis=0, keepdims=True))
    lb = e[0:1, :] / jnp.sum(e, axis=0, keepdims=True)

    q_ref[...] = _silu(proj(0)) * (HGRN_HEAD_DIM ** -0.5)
    fg = lb + (1.0 - lb) * jax.nn.sigmoid(proj(1))
    g_ref[...] = jnp.log(fg)
    k_ref[...] = 1.0 - fg
    v_ref[...] = proj(2).astype(_BF16)
    og_ref[...] = _silu(proj(3)).astype(_BF16)
    u_ref[...] = proj(4) * jax.nn.sigmoid(proj(5))
    ga_ref[...] = jax.nn.sigmoid(proj(6)).astype(_BF16)
    gb_ref[...] = jax.nn.sigmoid(proj(7)).astype(_BF16)


def _mix_in(x, mod, gain, w_in, hgrn_lb):
    B, S, D = x.shape
    tm = MIX_TOKENS
    tile = pl.BlockSpec((None, tm, D), lambda b, i: (b, i, 0))
    out_dtypes = (_F32, _F32, _F32, _BF16, _BF16, _F32, _BF16, _BF16)
    vmem = w_in.size * 2 + 2 * tm * D * 4 + sum(2 * tm * D * jnp.dtype(t).itemsize for t in out_dtypes) + 6 * tm * D * 4
    return pl.pallas_call(
        _mix_in_kernel,
        out_shape=[jax.ShapeDtypeStruct((B, S, D), t) for t in out_dtypes],
        grid=(B, S // tm),
        in_specs=[tile,
                  pl.BlockSpec((None, N_MOD, D), lambda b, i: (b, 0, 0)),
                  _resident((1, D)),
                  _resident(w_in.shape),
                  _resident(hgrn_lb.shape)],
        out_specs=[tile] * len(out_dtypes),
        compiler_params=pltpu.CompilerParams(dimension_semantics=("arbitrary", "arbitrary"),
                                             vmem_limit_bytes=_vmem_limit(vmem)),
        name="mix_in",
    )(x, mod, gain, w_in, hgrn_lb)


def _hgrn_prepare(q_ref, g_ref, k_ref, beta_ref, qd_ref, qs_ref, ki_ref, kb_ref, kd_ref, dec_ref):
    ts = q_ref.shape[0]
    C = HGRN_CHUNK
    half = C // 2
    n_chunks = ts // C
    row = lax.broadcasted_iota(jnp.int32, (C, C), 0)
    col = lax.broadcasted_iota(jnp.int32, (C, C), 1)
    tri = jnp.where((row >= col) & ((row >= half) == (col >= half)), 1.0, 0.0).astype(_BF16)
    first_half = lax.broadcasted_iota(jnp.int32, (C, 1), 0) < half
    worst = None
    for c in range(n_chunks):
        rows = slice(c * C, (c + 1) * C)
        g = g_ref[rows, :]
        g_hi = g.astype(_BF16)
        g_lo = (g - g_hi.astype(_F32)).astype(_BF16)
        beta = _dot(tri, g_hi) + _dot(tri, g_lo)
        l0 = beta[half - 1:half, :]
        l1 = beta[C - 1:C, :]
        both = jnp.minimum(l0, l1)
        worst = both if worst is None else jnp.minimum(worst, both)
        e0 = jnp.exp(l0)
        e1 = jnp.exp(l1)
        qd = q_ref[rows, :] * jnp.exp(beta)
        k = k_ref[rows, :]
        ki = k * jnp.exp(-beta)
        k_end = k * jnp.exp(jnp.where(first_half, l0, l1) - beta)
        beta_ref[rows, :] = beta
        qd_ref[rows, :] = qd.astype(_BF16)
        qs_ref[rows, :] = jnp.where(first_half, qd, qd * e0).astype(_BF16)
        ki_ref[rows, :] = ki.astype(_BF16)
        kb_ref[rows, :] = jnp.where(first_half, k_end, ki).astype(_BF16)
        kd_ref[rows, :] = jnp.where(first_half, k_end * e1, k_end).astype(_BF16)
        dec_ref[c:c + 1, :] = e0 * e1
        dec_ref[n_chunks + 2 * c:n_chunks + 2 * c + 1, :] = e0
        dec_ref[n_chunks + 2 * c + 1:n_chunks + 2 * c + 2, :] = e1
    return jnp.min(worst) >= -FACTORED_DECAY_LIMIT


def _hgrn_finish(o, lanes, gain_ref, og_ref, o_ref):
    o = o * lax.rsqrt(jnp.mean(o * o, axis=-1, keepdims=True) + EPS)
    o = o * gain_ref[:, lanes] * og_ref[:, lanes].astype(_F32)
    o_ref[:, lanes] = o.astype(o_ref.dtype)


def _hgrn_factored(v_ref, og_ref, gain_ref, o_ref, state_ref, qd_ref, qs_ref, ki_ref, kb_ref, kd_ref, dec_ref,
                   att_ref, upd_ref):
    ts, width = v_ref.shape
    hd = HGRN_HEAD_DIM
    C = HGRN_CHUNK
    half = C // 2
    n_chunks = ts // C
    causal = lax.broadcasted_iota(jnp.int32, (C, C), 0) >= lax.broadcasted_iota(jnp.int32, (C, C), 1)
    for h in range(width // hd):
        lanes = slice(h * hd, (h + 1) * hd)
        for c in range(n_chunks):
            rows = slice(c * C, (c + 1) * C)
            top = _dot_nt(qd_ref[c * C:c * C + half, lanes], ki_ref[rows, lanes])
            bottom = _dot_nt(qd_ref[c * C + half:(c + 1) * C, lanes], kb_ref[rows, lanes])
            att_ref[rows, lanes] = jnp.where(causal, jnp.concatenate([top, bottom], axis=0), 0.0).astype(_BF16)
            upd_ref[c, h] = _dot_tn(v_ref[rows, lanes], kd_ref[rows, lanes])
    for h in range(width // hd):
        lanes = slice(h * hd, (h + 1) * hd)
        state = state_ref[h]
        outs = []
        for c in range(n_chunks):
            rows = slice(c * C, (c + 1) * C)
            outs.append(_dot(att_ref[rows, lanes], v_ref[rows, lanes])
                        + _dot_nt(qs_ref[rows, lanes], state.astype(_BF16)))
            state = state * dec_ref[c:c + 1, lanes] + upd_ref[c, h]
        state_ref[h] = state
        _hgrn_finish(jnp.concatenate(outs, axis=0), lanes, gain_ref, og_ref, o_ref)


def _hgrn_exact(q_ref, k_ref, v_ref, og_ref, gain_ref, o_ref, state_ref, beta_ref, qd_ref, kb_ref, kd_ref, dec_ref,
                intra_ref):
    ts, width = v_ref.shape
    hd = HGRN_HEAD_DIM
    half = HGRN_CHUNK // 2
    n_chunks = ts // HGRN_CHUNK
    s_idx = lax.broadcasted_iota(jnp.int32, (half, 1), 0)

    def row_group(r, carry):
        base = pl.multiple_of(r * SUBLANES, SUBLANES)
        start = pl.multiple_of((base // half) * half, half)
        for h in range(width // hd):
            lanes = slice(h * hd, (h + 1) * hd)
            q8 = q_ref[pl.ds(base, SUBLANES), lanes]
            b8 = beta_ref[pl.ds(base, SUBLANES), lanes]
            k_c = k_ref[pl.ds(start, half), lanes]
            b_c = beta_ref[pl.ds(start, half), lanes]
            v_c = v_ref[pl.ds(start, half), lanes].astype(_F32)
            out_rows = []
            for i in range(SUBLANES):
                decay = jnp.exp(jnp.minimum(b8[i:i + 1, :] - b_c, 0.0))
                w = jnp.where(s_idx <= base - start + i, q8[i:i + 1, :] * k_c * decay, 0.0)
                att = jnp.sum(w, axis=-1, keepdims=True)
                out_rows.append(jnp.sum(att * v_c, axis=0, keepdims=True))
            intra_ref[pl.ds(base, SUBLANES), lanes] = jnp.concatenate(out_rows, axis=0)
        return carry

    lax.fori_loop(0, ts // SUBLANES, row_group, 0)
    for h in range(width // hd):
        lanes = slice(h * hd, (h + 1) * hd)
        state = state_ref[h]
        outs = []
        for s in range(2 * n_chunks):
            rows = slice(s * half, (s + 1) * half)
            k_end_ref = kb_ref if s % 2 == 0 else kd_ref
            outs.append(intra_ref[rows, lanes] + _dot_nt(qd_ref[rows, lanes], state.astype(_BF16)))
            state = state * dec_ref[n_chunks + s:n_chunks + s + 1, lanes] + _dot_tn(v_ref[rows, lanes],
                                                                                   k_end_ref[rows, lanes])
        state_ref[h] = state
        _hgrn_finish(jnp.concatenate(outs, axis=0), lanes, gain_ref, og_ref, o_ref)


def _hgrn_kernel(q_ref, g_ref, k_ref, v_ref, og_ref, gain_ref, o_ref,
                 state_ref, state_in_ref, beta_ref, qd_ref, qs_ref, ki_ref, kb_ref, kd_ref, dec_ref,
                 att_ref, upd_ref, intra_ref):
    @pl.when(pl.program_id(1) == 0)
    def _():
        state_ref[...] = jnp.zeros_like(state_ref)

    state_in_ref[...] = state_ref[...]
    factored_ok = _hgrn_prepare(q_ref, g_ref, k_ref, beta_ref, qd_ref, qs_ref, ki_ref, kb_ref, kd_ref, dec_ref)
    _hgrn_factored(v_ref, og_ref, gain_ref, o_ref, state_ref, qd_ref, qs_ref, ki_ref, kb_ref, kd_ref, dec_ref,
                   att_ref, upd_ref)

    @pl.when(jnp.logical_not(factored_ok))
    def _():
        state_ref[...] = state_in_ref[...]
        _hgrn_exact(q_ref, k_ref, v_ref, og_ref, gain_ref, o_ref, state_ref, beta_ref, qd_ref, kb_ref, kd_ref,
                    dec_ref, intra_ref)


def _hgrn_scratch(ts, width):
    n_heads = width // HGRN_HEAD_DIM
    n_chunks = ts // HGRN_CHUNK
    state = pltpu.VMEM((n_heads, HGRN_HEAD_DIM, HGRN_HEAD_DIM), _F32)
    operand = pltpu.VMEM((ts, width), _BF16)
    dec_rows = -(-3 * n_chunks // SUBLANES) * SUBLANES
    shapes = [state, state,
              pltpu.VMEM((ts, width), _F32),
              operand, operand, operand, operand, operand,
              pltpu.VMEM((dec_rows, width), _F32),
              operand,
              pltpu.VMEM((n_chunks, n_heads, HGRN_HEAD_DIM, HGRN_HEAD_DIM), _F32),
              pltpu.VMEM((ts, width), _F32)]
    nbytes = (2 + n_chunks) * n_heads * HGRN_HEAD_DIM ** 2 * 4 + ts * width * (2 * 4 + 6 * 2) + dec_rows * width * 4
    return shapes, nbytes


def _hgrn(q, g, k, v, og, gain):
    B, S, W = q.shape
    ts = MIX_TOKENS
    tile = pl.BlockSpec((None, ts, W), lambda b, i: (b, i, 0))
    scratch, scratch_bytes = _hgrn_scratch(ts, W)
    vmem = 2 * ts * W * (3 * 4 + 3 * 2) + scratch_bytes + 6 * ts * W * 4
    return pl.pallas_call(
        _hgrn_kernel,
        out_shape=jax.ShapeDtypeStruct((B, S, W), _BF16),
        grid=(B, S // ts),
        in_specs=[tile, tile, tile, tile, tile, _resident((1, W))],
        out_specs=tile,
        scratch_shapes=scratch,
        compiler_params=pltpu.CompilerParams(dimension_semantics=("arbitrary", "arbitrary"),
                                             vmem_limit_bytes=_vmem_limit(vmem)),
        name="hgrn",
    )(q, g, k, v, og, gain)


def _mix_out_kernel(x_ref, mod_ref, u_ref, ya_ref, ga_ref, gb_ref,
                    conv_w_ref, conv_b_ref, ln_g_ref, ln_b_ref, wa_ref, wb_ref, wo_ref,
                    o_ref, ubuf_ref, conv_ref):
    tm, D = x_ref.shape
    halo = CONV_HALO
    n_lane_tiles = D // LANES

    @pl.when(pl.program_id(1) == 0)
    def _():
        ubuf_ref[:, 0:halo, :] = jnp.zeros((n_lane_tiles, halo, LANES), _F32)

    first = halo - (CONV_K - 1)
    for l in range(n_lane_tiles):
        lanes = slice(l * LANES, (l + 1) * LANES)
        ubuf_ref[l, halo:, :] = u_ref[:, lanes]
        for r in range(0, tm, CONV_ROWS):
            acc = jnp.broadcast_to(conv_b_ref[:, lanes], (CONV_ROWS, LANES))
            for j in range(CONV_K):
                acc = acc + conv_w_ref[j:j + 1, lanes] * ubuf_ref[l, r + first + j:r + first + j + CONV_ROWS, :]
            conv_ref[r:r + CONV_ROWS, lanes] = acc
        ubuf_ref[l, 0:halo, :] = ubuf_ref[l, tm:tm + halo, :]

    y = conv_ref[...]
    mu = jnp.mean(y, axis=-1, keepdims=True)
    yc = y - mu
    yn = yc * lax.rsqrt(jnp.mean(yc * yc, axis=-1, keepdims=True) + EPS)
    z = _silu(yn * ln_g_ref[...] + ln_b_ref[...]).astype(_BF16)

    y_a = _dot(ya_ref[...], wa_ref[...])
    y_b = _dot(z, wb_ref[...])
    merged = (ga_ref[...].astype(_F32) * y_a + gb_ref[...].astype(_F32) * y_b).astype(_BF16)
    gate = mod_ref[pl.ds(5, 1), :]
    o_ref[...] = x_ref[...] + gate * _dot(merged, wo_ref[...])


def _mix_out(x, mod, u, ya, ga, gb, conv_w, conv_b, ln_g, ln_b, wa, wb, wo):
    B, S, D = x.shape
    tm = MIX_TOKENS
    tile = pl.BlockSpec((None, tm, D), lambda b, i: (b, i, 0))
    vmem = 3 * D * D * 2 + 2 * tm * D * (4 + 4 + 2 + 2 + 2 + 4) + (2 * tm + CONV_HALO) * D * 4 + 8 * tm * D * 4
    return pl.pallas_call(
        _mix_out_kernel,
        out_shape=jax.ShapeDtypeStruct((B, S, D), _F32),
        grid=(B, S // tm),
        in_specs=[tile,
                  pl.BlockSpec((None, N_MOD, D), lambda b, i: (b, 0, 0)),
                  tile, tile, tile, tile,
                  _resident(conv_w.shape), _resident((1, D)), _resident((1, D)), _resident((1, D)),
                  _resident(wa.shape), _resident(wb.shape), _resident(wo.shape)],
        out_specs=tile,
        scratch_shapes=[pltpu.VMEM((D // LANES, CONV_HALO + tm, LANES), _F32),
                        pltpu.VMEM((tm, D), _F32)],
        compiler_params=pltpu.CompilerParams(dimension_semantics=("arbitrary", "arbitrary"),
                                             vmem_limit_bytes=_vmem_limit(vmem)),
        name="mix_out",
    )(x, mod, u, ya, ga, gb, conv_w, conv_b, ln_g, ln_b, wa, wb, wo)


def _ffn_weights(w_in, w_out):
    D, two_f = w_in.shape
    F = two_f // 2
    fc = V7X_MXU_COL
    a = w_in[:, :F].reshape(D, F // fc, fc)
    b = w_in[:, F:].reshape(D, F // fc, fc)
    w = jnp.concatenate([a, b], axis=-1).transpose(1, 0, 2)
    return w.astype(_BF16), w_out.astype(_BF16)


def kernel(x, c, ada_w, ada_b, norm_ffn1, ffn1_w_in, ffn1_w_out, norm_mix, mix_w_in, hgrn_lb, hgrn_g, hgrn_w_o,
           conv_w, conv_b, conv_ln_g, conv_ln_b, conv_w_o, mix_w_out, norm_ffn2, ffn2_w_in, ffn2_w_out, norm_final):
    B, S, D = x.shape
    depth = ada_w.shape[0]
    assert depth == 1, "single-layer block"
    assert S % FFN_TOKENS == 0 and S % MIX_TOKENS == 0 and MIX_TOKENS % HGRN_CHUNK == 0 and HGRN_CHUNK % (2 * SUBLANES) == 0
    assert mix_w_in.shape[-1] == 8 * D and D % HGRN_HEAD_DIM == 0
    assert conv_w.shape[1] == CONV_K and CONV_K - 1 <= CONV_HALO and MIX_TOKENS % CONV_ROWS == 0

    mod = _adaln_mod(c, ada_w[0], ada_b).reshape(B, N_MOD, D)
    fin = norm_final.reshape(1, D)

    w_in1, w_out1 = _ffn_weights(ffn1_w_in[0], ffn1_w_out[0])
    x = _ffn(x, mod, norm_ffn1, w_in1, w_out1, fin, mod_row=0, final_norm=False)

    q, g, k, v, og, u, ga, gb = _mix_in(x, mod, norm_mix, mix_w_in[0].astype(_BF16), hgrn_lb)
    ya = _hgrn(q, g, k, v, og, hgrn_g)
    x = _mix_out(x, mod, u, ya, ga, gb, conv_w[0], conv_b, conv_ln_g, conv_ln_b,
                 hgrn_w_o[0].astype(_BF16), conv_w_o[0].astype(_BF16), mix_w_out[0].astype(_BF16))

    w_in2, w_out2 = _ffn_weights(ffn2_w_in[0], ffn2_w_out[0])
    return _ffn(x, mod, norm_ffn2, w_in2, w_out2, fin, mod_row=6, final_norm=True)
```

```python
import functools

import jax
import jax.numpy as jnp
from jax import lax
from jax.experimental import pallas as pl
from jax.experimental.pallas import tpu as pltpu

EPS = 1e-6
N_MOD = 9
HGRN_HEAD_DIM = 128
CONV_K = 31
FFN_RESIDUAL = 0.5

V7X_VMEM_BYTES = 64 * 1024 * 1024
V7X_MXU_COL = 256
SUBLANES = 8

FFN_TOKENS = 512
MIX_TOKENS = 256
HGRN_CHUNK = 128
LANES = 128
CONV_HALO = 32
CONV_ROWS = 64
FACTORED_DECAY_LIMIT = 80.0

_BF16 = jnp.bfloat16
_F32 = jnp.float32


def _vmem_limit(nbytes):
    return int(min(nbytes * 1.25 + (4 << 20), V7X_VMEM_BYTES - (8 << 20)))


def _resident(shape):
    nd = len(shape)
    return pl.BlockSpec(shape, lambda *_: (0,) * nd, pipeline_mode=pl.Buffered(1))


def _dot(a, b):
    return jnp.dot(a, b, preferred_element_type=_F32)


def _dot_nt(a, b):
    return lax.dot_general(a, b, (((1,), (1,)), ((), ())), preferred_element_type=_F32)


def _dot_tn(a, b):
    return lax.dot_general(a, b, (((0,), (0,)), ((), ())), preferred_element_type=_F32)


def _silu(x):
    return x * jax.nn.sigmoid(x)


def _modulated_rms_norm(x, gain, scale, shift):
    y = x * lax.rsqrt(jnp.mean(x * x, axis=-1, keepdims=True) + EPS)
    return (y * gain) * (1.0 + scale) + shift


def _adaln_kernel(c_ref, w_ref, b_ref, o_ref):
    cs = _silu(c_ref[...]).astype(_BF16)
    o_ref[...] = _dot(cs, w_ref[...].astype(_BF16)) + b_ref[...]


def _adaln_mod(c, ada_w, ada_b):
    B, D = c.shape
    n = ada_w.shape[1]
    return pl.pallas_call(
        _adaln_kernel,
        out_shape=jax.ShapeDtypeStruct((B, n), _F32),
        grid=(n // D,),
        in_specs=[pl.BlockSpec((B, D), lambda j: (0, 0)),
                  pl.BlockSpec((D, D), lambda j: (0, j)),
                  pl.BlockSpec((1, D), lambda j: (0, j))],
        out_specs=pl.BlockSpec((B, D), lambda j: (0, j)),
        compiler_params=pltpu.CompilerParams(dimension_semantics=("arbitrary",)),
        name="adaln_mod",
    )(c, ada_w, ada_b)


def _ffn_kernel(x_ref, mod_ref, gain_ref, w_in_ref, w_out_ref, fin_ref, o_ref, act_ref, *, mod_row, final_norm):
    x = x_ref[...]
    shift = mod_ref[pl.ds(mod_row, 1), :]
    scale = mod_ref[pl.ds(mod_row + 1, 1), :]
    gate = mod_ref[pl.ds(mod_row + 2, 1), :]
    h = _modulated_rms_norm(x, gain_ref[...], scale, shift).astype(_BF16)
    n_chunks = w_in_ref.shape[0]
    fc = w_in_ref.shape[2] // 2
    for j in range(n_chunks):
        ab = _dot(h, w_in_ref[j])
        a, b = ab[:, :fc], ab[:, fc:]
        act_ref[:, j * fc:(j + 1) * fc] = (_silu(a) * b).astype(_BF16)
    y = x + (FFN_RESIDUAL * gate) * _dot(act_ref[...], w_out_ref[...])
    if final_norm:
        y = y * lax.rsqrt(jnp.mean(y * y, axis=-1, keepdims=True) + EPS) * fin_ref[...]
    o_ref[...] = y


def _ffn(x, mod, gain, w_in, w_out, fin_gain, *, mod_row, final_norm):
    B, S, D = x.shape
    F = w_out.shape[0]
    tm = FFN_TOKENS
    vmem = (w_in.size + w_out.size) * 2 + 4 * tm * D * 4 + tm * F * 2 + 3 * tm * D * 4 + 2 * tm * w_in.shape[2] * 4
    return pl.pallas_call(
        functools.partial(_ffn_kernel, mod_row=mod_row, final_norm=final_norm),
        out_shape=jax.ShapeDtypeStruct((B, S, D), _F32),
        grid=(B, S // tm),
        in_specs=[pl.BlockSpec((None, tm, D), lambda b, i: (b, i, 0)),
                  pl.BlockSpec((None, N_MOD, D), lambda b, i: (b, 0, 0)),
                  _resident((1, D)),
                  _resident(w_in.shape),
                  _resident(w_out.shape),
                  _resident((1, D))],
        out_specs=pl.BlockSpec((None, tm, D), lambda b, i: (b, i, 0)),
        scratch_shapes=[pltpu.VMEM((tm, F), _BF16)],
        compiler_params=pltpu.CompilerParams(dimension_semantics=("arbitrary", "arbitrary"),
                                             vmem_limit_bytes=_vmem_limit(vmem)),
        name="ffn_final" if final_norm else "ffn",
    )(x, mod, gain, w_in, w_out, fin_gain)


def _project_hgrn(h, w_ref, lb_ref, q_ref, g_ref, k_ref, v_ref, og_ref):
    D = h.shape[-1]

    def proj(i):
        return _dot(h, w_ref[:, i * D:(i + 1) * D])

    lb_all = lb_ref[...]
    e = jnp.exp(lb_all - jnp.max(lb_all, axis=0, keepdims=True))
    lb = e[0:1, :] / jnp.sum(e, axis=0, keepdims=True)

    fg = lb + (1.0 - lb) * jax.nn.sigmoid(proj(1))
    g_ref[...] = jnp.log(fg)
    k_ref[...] = 1.0 - fg
    q_ref[...] = _silu(proj(0)) * (HGRN_HEAD_DIM ** -0.5)
    v_ref[...] = proj(2).astype(_BF16)
    og_ref[...] = _silu(proj(3)).astype(_BF16)


def _project_conv_input(h, w_ref, ubuf_ref):
    D = h.shape[-1]
    u = _dot(h, w_ref[:, 4 * D:5 * D]) * jax.nn.sigmoid(_dot(h, w_ref[:, 5 * D:6 * D]))
    for l in range(D // LANES):
        ubuf_ref[l, CONV_HALO:, :] = u[:, l * LANES:(l + 1) * LANES]


def _project_gates(h, w_ref, ga_ref, gb_ref):
    D = h.shape[-1]
    ga_ref[...] = jax.nn.sigmoid(_dot(h, w_ref[:, 6 * D:7 * D])).astype(_BF16)
    gb_ref[...] = jax.nn.sigmoid(_dot(h, w_ref[:, 7 * D:8 * D])).astype(_BF16)


def _hgrn_prepare(q_ref, g_ref, k_ref, beta_ref, qd_ref, qs_ref, ki_ref, kb_ref, kd_ref, dec_ref):
    ts = q_ref.shape[0]
    C = HGRN_CHUNK
    half = C // 2
    n_chunks = ts // C
    row = lax.broadcasted_iota(jnp.int32, (C, C), 0)
    col = lax.broadcasted_iota(jnp.int32, (C, C), 1)
    tri = jnp.where((row >= col) & ((row >= half) == (col >= half)), 1.0, 0.0).astype(_BF16)
    first_half = lax.broadcasted_iota(jnp.int32, (C, 1), 0) < half
    worst = None
    for c in range(n_chunks):
        rows = slice(c * C, (c + 1) * C)
        g = g_ref[rows, :]
        g_hi = g.astype(_BF16)
        g_lo = (g - g_hi.astype(_F32)).astype(_BF16)
        beta = _dot(tri, g_hi) + _dot(tri, g_lo)
        l0 = beta[half - 1:half, :]
        l1 = beta[C - 1:C, :]
        both = jnp.minimum(l0, l1)
        worst = both if worst is None else jnp.minimum(worst, both)
        e0 = jnp.exp(l0)
        e1 = jnp.exp(l1)
        qd = q_ref[rows, :] * jnp.exp(beta)
        k = k_ref[rows, :]
        ki = k * jnp.exp(-beta)
        k_end = k * jnp.exp(jnp.where(first_half, l0, l1) - beta)
        beta_ref[rows, :] = beta
        qd_ref[rows, :] = qd.astype(_BF16)
        qs_ref[rows, :] = jnp.where(first_half, qd, qd * e0).astype(_BF16)
        ki_ref[rows, :] = ki.astype(_BF16)
        kb_ref[rows, :] = jnp.where(first_half, k_end, ki).astype(_BF16)
        kd_ref[rows, :] = jnp.where(first_half, k_end * e1, k_end).astype(_BF16)
        dec_ref[c:c + 1, :] = e0 * e1
        dec_ref[n_chunks + 2 * c:n_chunks + 2 * c + 1, :] = e0
        dec_ref[n_chunks + 2 * c + 1:n_chunks + 2 * c + 2, :] = e1
    return jnp.min(worst) >= -FACTORED_DECAY_LIMIT


def _hgrn_finish(o, lanes, gain_ref, og_ref, o_ref):
    o = o * lax.rsqrt(jnp.mean(o * o, axis=-1, keepdims=True) + EPS)
    o = o * gain_ref[:, lanes] * og_ref[:, lanes].astype(_F32)
    o_ref[:, lanes] = o.astype(o_ref.dtype)


def _hgrn_factored(v_ref, og_ref, gain_ref, o_ref, state_ref, qd_ref, qs_ref, ki_ref, kb_ref, kd_ref, dec_ref,
                   att_ref, upd_ref):
    ts, width = v_ref.shape
    hd = HGRN_HEAD_DIM
    C = HGRN_CHUNK
    half = C // 2
    n_chunks = ts // C
    causal = lax.broadcasted_iota(jnp.int32, (C, C), 0) >= lax.broadcasted_iota(jnp.int32, (C, C), 1)
    for h in range(width // hd):
        lanes = slice(h * hd, (h + 1) * hd)
        for c in range(n_chunks):
            rows = slice(c * C, (c + 1) * C)
            top = _dot_nt(qd_ref[c * C:c * C + half, lanes], ki_ref[rows, lanes])
            bottom = _dot_nt(qd_ref[c * C + half:(c + 1) * C, lanes], kb_ref[rows, lanes])
            att_ref[rows, lanes] = jnp.where(causal, jnp.concatenate([top, bottom], axis=0), 0.0).astype(_BF16)
            upd_ref[c, h] = _dot_tn(v_ref[rows, lanes], kd_ref[rows, lanes])
    for h in range(width // hd):
        lanes = slice(h * hd, (h + 1) * hd)
        state = state_ref[h]
        outs = []
        for c in range(n_chunks):
            rows = slice(c * C, (c + 1) * C)
            outs.append(_dot(att_ref[rows, lanes], v_ref[rows, lanes])
                        + _dot_nt(qs_ref[rows, lanes], state.astype(_BF16)))
            state = state * dec_ref[c:c + 1, lanes] + upd_ref[c, h]
        state_ref[h] = state
        _hgrn_finish(jnp.concatenate(outs, axis=0), lanes, gain_ref, og_ref, o_ref)


def _hgrn_exact(q_ref, k_ref, v_ref, og_ref, gain_ref, o_ref, state_ref, beta_ref, qd_ref, kb_ref, kd_ref, dec_ref,
                intra_ref):
    ts, width = v_ref.shape
    hd = HGRN_HEAD_DIM
    half = HGRN_CHUNK // 2
    n_chunks = ts // HGRN_CHUNK
    s_idx = lax.broadcasted_iota(jnp.int32, (half, 1), 0)

    def row_group(r, carry):
        base = pl.multiple_of(r * SUBLANES, SUBLANES)
        start = pl.multiple_of((base // half) * half, half)
        for h in range(width // hd):
            lanes = slice(h * hd, (h + 1) * hd)
            q8 = q_ref[pl.ds(base, SUBLANES), lanes]
            b8 = beta_ref[pl.ds(base, SUBLANES), lanes]
            k_c = k_ref[pl.ds(start, half), lanes]
            b_c = beta_ref[pl.ds(start, half), lanes]
            v_c = v_ref[pl.ds(start, half), lanes].astype(_F32)
            out_rows = []
            for i in range(SUBLANES):
                decay = jnp.exp(jnp.minimum(b8[i:i + 1, :] - b_c, 0.0))
                w = jnp.where(s_idx <= base - start + i, q8[i:i + 1, :] * k_c * decay, 0.0)
                att = jnp.sum(w, axis=-1, keepdims=True)
                out_rows.append(jnp.sum(att * v_c, axis=0, keepdims=True))
            intra_ref[pl.ds(base, SUBLANES), lanes] = jnp.concatenate(out_rows, axis=0)
        return carry

    lax.fori_loop(0, ts // SUBLANES, row_group, 0)
    for h in range(width // hd):
        lanes = slice(h * hd, (h + 1) * hd)
        state = state_ref[h]
        outs = []
        for s in range(2 * n_chunks):
            rows = slice(s * half, (s + 1) * half)
            k_end_ref = kb_ref if s % 2 == 0 else kd_ref
            outs.append(intra_ref[rows, lanes] + _dot_nt(qd_ref[rows, lanes], state.astype(_BF16)))
            state = state * dec_ref[n_chunks + s:n_chunks + s + 1, lanes] + _dot_tn(v_ref[rows, lanes],
                                                                                   k_end_ref[rows, lanes])
        state_ref[h] = state
        _hgrn_finish(jnp.concatenate(outs, axis=0), lanes, gain_ref, og_ref, o_ref)


def _conv_branch(ubuf_ref, conv_w_ref, conv_b_ref, ln_g_ref, ln_b_ref, conv_ref, z_ref):
    tm = conv_ref.shape[0]
    first = CONV_HALO - (CONV_K - 1)
    for l in range(ubuf_ref.shape[0]):
        lanes = slice(l * LANES, (l + 1) * LANES)
        for r in range(0, tm, CONV_ROWS):
            acc = jnp.broadcast_to(conv_b_ref[:, lanes], (CONV_ROWS, LANES))
            for j in range(CONV_K):
                acc = acc + conv_w_ref[j:j + 1, lanes] * ubuf_ref[l, r + first + j:r + first + j + CONV_ROWS, :]
            conv_ref[r:r + CONV_ROWS, lanes] = acc
        ubuf_ref[l, 0:CONV_HALO, :] = ubuf_ref[l, tm:tm + CONV_HALO, :]

    y = conv_ref[...]
    mu = jnp.mean(y, axis=-1, keepdims=True)
    yc = y - mu
    yn = yc * lax.rsqrt(jnp.mean(yc * yc, axis=-1, keepdims=True) + EPS)
    z_ref[...] = _silu(yn * ln_g_ref[...] + ln_b_ref[...]).astype(_BF16)


def _merge_and_project(x_ref, gate, ya_ref, z_ref, ga_ref, gb_ref, wa_ref, wb_ref, wo_ref, o_ref):
    y_a = _dot(ya_ref[...], wa_ref[...])
    y_b = _dot(z_ref[...], wb_ref[...])
    merged = (ga_ref[...].astype(_F32) * y_a + gb_ref[...].astype(_F32) * y_b).astype(_BF16)
    o_ref[...] = x_ref[...] + gate * _dot(merged, wo_ref[...])


def _mixer_kernel(x_ref, mod_ref, gain_ref, w_in_ref, lb_ref, hgrn_gain_ref,
                  conv_w_ref, conv_b_ref, ln_g_ref, ln_b_ref, wa_ref, wb_ref, wo_ref,
                  o_ref,
                  q_ref, g_ref, k_ref, v_ref, og_ref, ga_ref, gb_ref, ya_ref, ubuf_ref, conv_ref, z_ref,
                  state_ref, state_in_ref, beta_ref, qd_ref, qs_ref, ki_ref, kb_ref, kd_ref, dec_ref,
                  att_ref, upd_ref, intra_ref):
    @pl.when(pl.program_id(1) == 0)
    def _():
        state_ref[...] = jnp.zeros_like(state_ref)
        ubuf_ref[:, 0:CONV_HALO, :] = jnp.zeros((ubuf_ref.shape[0], CONV_HALO, LANES), _F32)

    shift = mod_ref[pl.ds(3, 1), :]
    scale = mod_ref[pl.ds(4, 1), :]
    gate = mod_ref[pl.ds(5, 1), :]
    h = _modulated_rms_norm(x_ref[...], gain_ref[...], scale, shift).astype(_BF16)

    _project_conv_input(h, w_in_ref, ubuf_ref)
    _conv_branch(ubuf_ref, conv_w_ref, conv_b_ref, ln_g_ref, ln_b_ref, conv_ref, z_ref)
    _project_hgrn(h, w_in_ref, lb_ref, q_ref, g_ref, k_ref, v_ref, og_ref)
    state_in_ref[...] = state_ref[...]
    factored_ok = _hgrn_prepare(q_ref, g_ref, k_ref, beta_ref, qd_ref, qs_ref, ki_ref, kb_ref, kd_ref, dec_ref)
    _project_gates(h, w_in_ref, ga_ref, gb_ref)
    _hgrn_factored(v_ref, og_ref, hgrn_gain_ref, ya_ref, state_ref, qd_ref, qs_ref, ki_ref, kb_ref, kd_ref, dec_ref,
                   att_ref, upd_ref)
    _merge_and_project(x_ref, gate, ya_ref, z_ref, ga_ref, gb_ref, wa_ref, wb_ref, wo_ref, o_ref)

    @pl.when(jnp.logical_not(factored_ok))
    def _():
        state_ref[...] = state_in_ref[...]
        _hgrn_exact(q_ref, k_ref, v_ref, og_ref, hgrn_gain_ref, ya_ref, state_ref, beta_ref, qd_ref, kb_ref, kd_ref,
                    dec_ref, intra_ref)
        _merge_and_project(x_ref, gate, ya_ref, z_ref, ga_ref, gb_ref, wa_ref, wb_ref, wo_ref, o_ref)


def _mixer(x, mod, gain, w_in, hgrn_lb, hgrn_gain, conv_w, conv_b, ln_g, ln_b, wa, wb, wo):
    B, S, D = x.shape
    tm = MIX_TOKENS
    n_heads = D // HGRN_HEAD_DIM
    n_chunks = tm // HGRN_CHUNK
    tile = pl.BlockSpec((None, tm, D), lambda b, i: (b, i, 0))
    f32_tile = pltpu.VMEM((tm, D), _F32)
    bf16_tile = pltpu.VMEM((tm, D), _BF16)
    state = pltpu.VMEM((n_heads, HGRN_HEAD_DIM, HGRN_HEAD_DIM), _F32)
    dec_rows = -(-3 * n_chunks // SUBLANES) * SUBLANES
    scratch = [f32_tile, f32_tile, f32_tile, bf16_tile, bf16_tile, bf16_tile, bf16_tile,
               bf16_tile,
               pltpu.VMEM((D // LANES, CONV_HALO + tm, LANES), _F32), f32_tile, bf16_tile,
               state, state, f32_tile,
               bf16_tile, bf16_tile, bf16_tile, bf16_tile, bf16_tile,
               pltpu.VMEM((dec_rows, D), _F32), bf16_tile,
               pltpu.VMEM((n_chunks, n_heads, HGRN_HEAD_DIM, HGRN_HEAD_DIM), _F32), f32_tile]
    weights = (w_in.size + wa.size + wb.size + wo.size) * 2
    vmem = (weights + 4 * tm * D * 4 + 7 * tm * D * 4 + 12 * tm * D * 2 + (CONV_HALO + tm) * D * 4
            + (2 + n_chunks) * n_heads * HGRN_HEAD_DIM ** 2 * 4 + 8 * tm * D * 4)
    return pl.pallas_call(
        _mixer_kernel,
        out_shape=jax.ShapeDtypeStruct((B, S, D), _F32),
        grid=(B, S // tm),
        in_specs=[tile,
                  pl.BlockSpec((None, N_MOD, D), lambda b, i: (b, 0, 0)),
                  _resident((1, D)), _resident(w_in.shape), _resident(hgrn_lb.shape), _resident((1, D)),
                  _resident(conv_w.shape), _resident((1, D)), _resident((1, D)), _resident((1, D)),
                  _resident(wa.shape), _resident(wb.shape), _resident(wo.shape)],
        out_specs=tile,
        scratch_shapes=scratch,
        compiler_params=pltpu.CompilerParams(dimension_semantics=("arbitrary", "arbitrary"),
                                             vmem_limit_bytes=_vmem_limit(vmem)),
        name="mixer",
    )(x, mod, gain, w_in, hgrn_lb, hgrn_gain, conv_w, conv_b, ln_g, ln_b, wa, wb, wo)


def _ffn_weights(w_in, w_out):
    D, two_f = w_in.shape
    F = two_f // 2
    fc = V7X_MXU_COL
    a = w_in[:, :F].reshape(D, F // fc, fc)
    b = w_in[:, F:].reshape(D, F // fc, fc)
    w = jnp.concatenate([a, b], axis=-1).transpose(1, 0, 2)
    return w.astype(_BF16), w_out.astype(_BF16)


def kernel(x, c, ada_w, ada_b, norm_ffn1, ffn1_w_in, ffn1_w_out, norm_mix, mix_w_in, hgrn_lb, hgrn_g, hgrn_w_o,
           conv_w, conv_b, conv_ln_g, conv_ln_b, conv_w_o, mix_w_out, norm_ffn2, ffn2_w_in, ffn2_w_out, norm_final):
    B, S, D = x.shape
    depth = ada_w.shape[0]
    assert depth == 1, "single-layer block"
    assert S % FFN_TOKENS == 0 and S % MIX_TOKENS == 0 and MIX_TOKENS % HGRN_CHUNK == 0 and HGRN_CHUNK % (2 * SUBLANES) == 0
    assert mix_w_in.shape[-1] == 8 * D and D % HGRN_HEAD_DIM == 0
    assert conv_w.shape[1] == CONV_K and CONV_K - 1 <= CONV_HALO and MIX_TOKENS % CONV_ROWS == 0

    mod = _adaln_mod(c, ada_w[0], ada_b).reshape(B, N_MOD, D)
    fin = norm_final.reshape(1, D)

    w_in1, w_out1 = _ffn_weights(ffn1_w_in[0], ffn1_w_out[0])
    x = _ffn(x, mod, norm_ffn1, w_in1, w_out1, fin, mod_row=0, final_norm=False)

    x = _mixer(x, mod, norm_mix, mix_w_in[0].astype(_BF16), hgrn_lb, hgrn_g, conv_w[0], conv_b, conv_ln_g, conv_ln_b,
               hgrn_w_o[0].astype(_BF16), conv_w_o[0].astype(_BF16), mix_w_out[0].astype(_BF16))

    w_in2, w_out2 = _ffn_weights(ffn2_w_in[0], ffn2_w_out[0])
    return _ffn(x, mod, norm_ffn2, w_in2, w_out2, fin, mod_row=6, final_norm=True)
```

```python
import functools

import jax
import jax.numpy as jnp
from jax import lax
from jax.experimental import pallas as pl
from jax.experimental.pallas import tpu as pltpu

EPS = 1e-6
N_MOD = 9
HGRN_HEAD_DIM = 128
CONV_K = 31
FFN_RESIDUAL = 0.5

V7X_VMEM_BYTES = 64 * 1024 * 1024
V7X_MXU_COL = 256
SUBLANES = 8

ADALN_COLS = 3072
FFN_TOKENS = 512
MIX_TOKENS = 256
HGRN_CHUNK = 128
LANES = 128
CONV_HALO = 32
CONV_ROWS = 64
FACTORED_DECAY_LIMIT = 80.0

_BF16 = jnp.bfloat16
_F32 = jnp.float32


def _vmem_limit(nbytes):
    return int(min(nbytes * 1.25 + (4 << 20), V7X_VMEM_BYTES - (8 << 20)))


def _resident(shape):
    nd = len(shape)
    return pl.BlockSpec(shape, lambda *_: (0,) * nd, pipeline_mode=pl.Buffered(1))


def _dot(a, b):
    return jnp.dot(a, b, preferred_element_type=_F32)


def _dot_nt(a, b):
    return lax.dot_general(a, b, (((1,), (1,)), ((), ())), preferred_element_type=_F32)


def _dot_tn(a, b):
    return lax.dot_general(a, b, (((0,), (0,)), ((), ())), preferred_element_type=_F32)


def _silu(x):
    return x * jax.nn.sigmoid(x)


def _modulated_rms_norm(x, gain, scale, shift):
    y = x * lax.rsqrt(jnp.mean(x * x, axis=-1, keepdims=True) + EPS)
    return (y * gain) * (1.0 + scale) + shift


def _adaln_kernel(c_ref, w_ref, b_ref, o_ref):
    cs = _silu(c_ref[...]).astype(_BF16)
    o_ref[...] = _dot(cs, w_ref[...].astype(_BF16)) + b_ref[...]


def _adaln_mod(c, ada_w, ada_b):
    B, D = c.shape
    n = ada_w.shape[1]
    cols = ADALN_COLS
    assert n % cols == 0
    vmem = 2 * D * cols * 4 + D * cols * 2 + 4 * B * cols * 4
    return pl.pallas_call(
        _adaln_kernel,
        out_shape=jax.ShapeDtypeStruct((B, n), _F32),
        grid=(n // cols,),
        in_specs=[pl.BlockSpec((B, D), lambda j: (0, 0)),
                  pl.BlockSpec((D, cols), lambda j: (0, j)),
                  pl.BlockSpec((1, cols), lambda j: (0, j))],
        out_specs=pl.BlockSpec((B, cols), lambda j: (0, j)),
        compiler_params=pltpu.CompilerParams(dimension_semantics=("arbitrary",),
                                             vmem_limit_bytes=_vmem_limit(vmem)),
        name="adaln_mod",
    )(c, ada_w, ada_b)


def _ffn_kernel(x_ref, mod_ref, gain_ref, w_in_ref, w_out_ref, fin_ref, o_ref, act_ref, *, mod_row, final_norm):
    x = x_ref[...]
    shift = mod_ref[pl.ds(mod_row, 1), :]
    scale = mod_ref[pl.ds(mod_row + 1, 1), :]
    gate = mod_ref[pl.ds(mod_row + 2, 1), :]
    h = _modulated_rms_norm(x, gain_ref[...], scale, shift).astype(_BF16)
    F = w_out_ref.shape[0]
    fc = V7X_MXU_COL
    for j in range(F // fc):
        a = _dot(h, w_in_ref[:, j * fc:(j + 1) * fc])
        b = _dot(h, w_in_ref[:, F + j * fc:F + (j + 1) * fc])
        act_ref[:, j * fc:(j + 1) * fc] = (_silu(a) * b).astype(_BF16)
    y = x + (FFN_RESIDUAL * gate) * _dot(act_ref[...], w_out_ref[...])
    if final_norm:
        y = y * lax.rsqrt(jnp.mean(y * y, axis=-1, keepdims=True) + EPS) * fin_ref[...]
    o_ref[...] = y


def _ffn(x, mod, gain, w_in, w_out, fin_gain, *, mod_row, final_norm):
    B, S, D = x.shape
    F = w_out.shape[0]
    assert w_in.shape == (D, 2 * F) and F % V7X_MXU_COL == 0
    tm = FFN_TOKENS
    vmem = (w_in.size + w_out.size) * 2 + 4 * tm * D * 4 + tm * F * 2 + 3 * tm * D * 4 + 4 * tm * V7X_MXU_COL * 4
    return pl.pallas_call(
        functools.partial(_ffn_kernel, mod_row=mod_row, final_norm=final_norm),
        out_shape=jax.ShapeDtypeStruct((B, S, D), _F32),
        grid=(B, S // tm),
        in_specs=[pl.BlockSpec((None, tm, D), lambda b, i: (b, i, 0)),
                  pl.BlockSpec((None, N_MOD, D), lambda b, i: (b, 0, 0)),
                  _resident((1, D)),
                  _resident(w_in.shape),
                  _resident(w_out.shape),
                  _resident((1, D))],
        out_specs=pl.BlockSpec((None, tm, D), lambda b, i: (b, i, 0)),
        scratch_shapes=[pltpu.VMEM((tm, F), _BF16)],
        compiler_params=pltpu.CompilerParams(dimension_semantics=("arbitrary", "arbitrary"),
                                             vmem_limit_bytes=_vmem_limit(vmem)),
        name="ffn_final" if final_norm else "ffn",
    )(x, mod, gain, w_in, w_out, fin_gain)


def _project_hgrn(h, w_ref, lb_ref, q_ref, g_ref, k_ref, v_ref, og_ref):
    D = h.shape[-1]

    def proj(i):
        return _dot(h, w_ref[:, i * D:(i + 1) * D])

    lb_all = lb_ref[...]
    e = jnp.exp(lb_all - jnp.max(lb_all, axis=0, keepdims=True))
    lb = e[0:1, :] / jnp.sum(e, axis=0, keepdims=True)

    fg = lb + (1.0 - lb) * jax.nn.sigmoid(proj(1))
    g_ref[...] = jnp.log(fg)
    k_ref[...] = 1.0 - fg
    q_ref[...] = _silu(proj(0)) * (HGRN_HEAD_DIM ** -0.5)
    v_ref[...] = proj(2).astype(_BF16)
    og_ref[...] = _silu(proj(3)).astype(_BF16)


def _project_conv_input(h, w_ref, ubuf_ref):
    D = h.shape[-1]
    u = _dot(h, w_ref[:, 4 * D:5 * D]) * jax.nn.sigmoid(_dot(h, w_ref[:, 5 * D:6 * D]))
    for l in range(D // LANES):
        ubuf_ref[l, CONV_HALO:, :] = u[:, l * LANES:(l + 1) * LANES]


def _project_gates(h, w_ref, ga_ref, gb_ref):
    D = h.shape[-1]
    ga_ref[...] = jax.nn.sigmoid(_dot(h, w_ref[:, 6 * D:7 * D])).astype(_BF16)
    gb_ref[...] = jax.nn.sigmoid(_dot(h, w_ref[:, 7 * D:8 * D])).astype(_BF16)


def _hgrn_prepare(q_ref, g_ref, k_ref, beta_ref, qd_ref, qs_ref, ki_ref, kb_ref, kd_ref, dec_ref):
    ts = q_ref.shape[0]
    C = HGRN_CHUNK
    half = C // 2
    n_chunks = ts // C
    row = lax.broadcasted_iota(jnp.int32, (C, C), 0)
    col = lax.broadcasted_iota(jnp.int32, (C, C), 1)
    tri = jnp.where((row >= col) & ((row >= half) == (col >= half)), 1.0, 0.0).astype(_BF16)
    first_half = lax.broadcasted_iota(jnp.int32, (C, 1), 0) < half
    worst = None
    for c in range(n_chunks):
        rows = slice(c * C, (c + 1) * C)
        g = g_ref[rows, :]
        g_hi = g.astype(_BF16)
        g_lo = (g - g_hi.astype(_F32)).astype(_BF16)
        beta = _dot(tri, g_hi) + _dot(tri, g_lo)
        l0 = beta[half - 1:half, :]
        l1 = beta[C - 1:C, :]
        both = jnp.minimum(l0, l1)
        worst = both if worst is None else jnp.minimum(worst, both)
        e0 = jnp.exp(l0)
        e1 = jnp.exp(l1)
        qd = q_ref[rows, :] * jnp.exp(beta)
        k = k_ref[rows, :]
        ki = k * jnp.exp(-beta)
        k_end = k * jnp.exp(jnp.where(first_half, l0, l1) - beta)
        beta_ref[rows, :] = beta
        qd_ref[rows, :] = qd.astype(_BF16)
        qs_ref[rows, :] = jnp.where(first_half, qd, qd * e0).astype(_BF16)
        ki_ref[rows, :] = ki.astype(_BF16)
        kb_ref[rows, :] = jnp.where(first_half, k_end, ki).astype(_BF16)
        kd_ref[rows, :] = jnp.where(first_half, k_end * e1, k_end).astype(_BF16)
        dec_ref[c:c + 1, :] = e0 * e1
        dec_ref[n_chunks + 2 * c:n_chunks + 2 * c + 1, :] = e0
        dec_ref[n_chunks + 2 * c + 1:n_chunks + 2 * c + 2, :] = e1
    return jnp.min(worst) >= -FACTORED_DECAY_LIMIT


def _hgrn_finish(o, lanes, gain_ref, og_ref, o_ref):
    o = o * lax.rsqrt(jnp.mean(o * o, axis=-1, keepdims=True) + EPS)
    o = o * gain_ref[:, lanes] * og_ref[:, lanes].astype(_F32)
    o_ref[:, lanes] = o.astype(o_ref.dtype)


def _hgrn_factored(v_ref, og_ref, gain_ref, o_ref, state_ref, qd_ref, qs_ref, ki_ref, kb_ref, kd_ref, dec_ref,
                   att_ref, upd_ref):
    ts, width = v_ref.shape
    hd = HGRN_HEAD_DIM
    C = HGRN_CHUNK
    half = C // 2
    n_chunks = ts // C
    causal = lax.broadcasted_iota(jnp.int32, (C, C), 0) >= lax.broadcasted_iota(jnp.int32, (C, C), 1)
    for h in range(width // hd):
        lanes = slice(h * hd, (h + 1) * hd)
        for c in range(n_chunks):
            rows = slice(c * C, (c + 1) * C)
            top = _dot_nt(qd_ref[c * C:c * C + half, lanes], ki_ref[rows, lanes])
            bottom = _dot_nt(qd_ref[c * C + half:(c + 1) * C, lanes], kb_ref[rows, lanes])
            att_ref[rows, lanes] = jnp.where(causal, jnp.concatenate([top, bottom], axis=0), 0.0).astype(_BF16)
            upd_ref[c, h] = _dot_tn(v_ref[rows, lanes], kd_ref[rows, lanes])
    for h in range(width // hd):
        lanes = slice(h * hd, (h + 1) * hd)
        state = state_ref[h]
        outs = []
        for c in range(n_chunks):
            rows = slice(c * C, (c + 1) * C)
            outs.append(_dot(att_ref[rows, lanes], v_ref[rows, lanes])
                        + _dot_nt(qs_ref[rows, lanes], state.astype(_BF16)))
            state = state * dec_ref[c:c + 1, lanes] + upd_ref[c, h]
        state_ref[h] = state
        _hgrn_finish(jnp.concatenate(outs, axis=0), lanes, gain_ref, og_ref, o_ref)


def _hgrn_exact(q_ref, k_ref, v_ref, og_ref, gain_ref, o_ref, state_ref, beta_ref, qd_ref, kb_ref, kd_ref, dec_ref,
                intra_ref):
    ts, width = v_ref.shape
    hd = HGRN_HEAD_DIM
    half = HGRN_CHUNK // 2
    n_chunks = ts // HGRN_CHUNK
    s_idx = lax.broadcasted_iota(jnp.int32, (half, 1), 0)

    def row_group(r, carry):
        base = pl.multiple_of(r * SUBLANES, SUBLANES)
        start = pl.multiple_of((base // half) * half, half)
        for h in range(width // hd):
            lanes = slice(h * hd, (h + 1) * hd)
            q8 = q_ref[pl.ds(base, SUBLANES), lanes]
            b8 = beta_ref[pl.ds(base, SUBLANES), lanes]
            k_c = k_ref[pl.ds(start, half), lanes]
            b_c = beta_ref[pl.ds(start, half), lanes]
            v_c = v_ref[pl.ds(start, half), lanes].astype(_F32)
            out_rows = []
            for i in range(SUBLANES):
                decay = jnp.exp(jnp.minimum(b8[i:i + 1, :] - b_c, 0.0))
                w = jnp.where(s_idx <= base - start + i, q8[i:i + 1, :] * k_c * decay, 0.0)
                att = jnp.sum(w, axis=-1, keepdims=True)
                out_rows.append(jnp.sum(att * v_c, axis=0, keepdims=True))
            intra_ref[pl.ds(base, SUBLANES), lanes] = jnp.concatenate(out_rows, axis=0)
        return carry

    lax.fori_loop(0, ts // SUBLANES, row_group, 0)
    for h in range(width // hd):
        lanes = slice(h * hd, (h + 1) * hd)
        state = state_ref[h]
        outs = []
        for s in range(2 * n_chunks):
            rows = slice(s * half, (s + 1) * half)
            k_end_ref = kb_ref if s % 2 == 0 else kd_ref
            outs.append(intra_ref[rows, lanes] + _dot_nt(qd_ref[rows, lanes], state.astype(_BF16)))
            state = state * dec_ref[n_chunks + s:n_chunks + s + 1, lanes] + _dot_tn(v_ref[rows, lanes],
                                                                                   k_end_ref[rows, lanes])
        state_ref[h] = state
        _hgrn_finish(jnp.concatenate(outs, axis=0), lanes, gain_ref, og_ref, o_ref)


def _conv_branch(ubuf_ref, conv_w_ref, conv_b_ref, ln_g_ref, ln_b_ref, conv_ref, z_ref):
    tm = conv_ref.shape[0]
    first = CONV_HALO - (CONV_K - 1)
    for l in range(ubuf_ref.shape[0]):
        lanes = slice(l * LANES, (l + 1) * LANES)
        for r in range(0, tm, CONV_ROWS):
            acc = jnp.broadcast_to(conv_b_ref[:, lanes], (CONV_ROWS, LANES))
            for j in range(CONV_K):
                acc = acc + conv_w_ref[j:j + 1, lanes] * ubuf_ref[l, r + first + j:r + first + j + CONV_ROWS, :]
            conv_ref[r:r + CONV_ROWS, lanes] = acc
        ubuf_ref[l, 0:CONV_HALO, :] = ubuf_ref[l, tm:tm + CONV_HALO, :]

    y = conv_ref[...]
    mu = jnp.mean(y, axis=-1, keepdims=True)
    yc = y - mu
    yn = yc * lax.rsqrt(jnp.mean(yc * yc, axis=-1, keepdims=True) + EPS)
    z_ref[...] = _silu(yn * ln_g_ref[...] + ln_b_ref[...]).astype(_BF16)


def _merge_and_project(x_ref, gate, ya_ref, z_ref, ga_ref, gb_ref, wa_ref, wb_ref, wo_ref, o_ref):
    y_a = _dot(ya_ref[...], wa_ref[...])
    y_b = _dot(z_ref[...], wb_ref[...])
    merged = (ga_ref[...].astype(_F32) * y_a + gb_ref[...].astype(_F32) * y_b).astype(_BF16)
    o_ref[...] = x_ref[...] + gate * _dot(merged, wo_ref[...])


def _mixer_kernel(x_ref, mod_ref, gain_ref, w_in_ref, lb_ref, hgrn_gain_ref,
                  conv_w_ref, conv_b_ref, ln_g_ref, ln_b_ref, wa_ref, wb_ref, wo_ref,
                  o_ref,
                  q_ref, g_ref, k_ref, v_ref, og_ref, ga_ref, gb_ref, ya_ref, ubuf_ref, conv_ref, z_ref,
                  state_ref, state_in_ref, beta_ref, qd_ref, qs_ref, ki_ref, kb_ref, kd_ref, dec_ref,
                  att_ref, upd_ref, intra_ref):
    @pl.when(pl.program_id(1) == 0)
    def _():
        state_ref[...] = jnp.zeros_like(state_ref)
        ubuf_ref[:, 0:CONV_HALO, :] = jnp.zeros((ubuf_ref.shape[0], CONV_HALO, LANES), _F32)

    shift = mod_ref[pl.ds(3, 1), :]
    scale = mod_ref[pl.ds(4, 1), :]
    gate = mod_ref[pl.ds(5, 1), :]
    h = _modulated_rms_norm(x_ref[...], gain_ref[...], scale, shift).astype(_BF16)

    _project_conv_input(h, w_in_ref, ubuf_ref)
    _conv_branch(ubuf_ref, conv_w_ref, conv_b_ref, ln_g_ref, ln_b_ref, conv_ref, z_ref)
    _project_hgrn(h, w_in_ref, lb_ref, q_ref, g_ref, k_ref, v_ref, og_ref)
    state_in_ref[...] = state_ref[...]
    factored_ok = _hgrn_prepare(q_ref, g_ref, k_ref, beta_ref, qd_ref, qs_ref, ki_ref, kb_ref, kd_ref, dec_ref)
    _project_gates(h, w_in_ref, ga_ref, gb_ref)
    _hgrn_factored(v_ref, og_ref, hgrn_gain_ref, ya_ref, state_ref, qd_ref, qs_ref, ki_ref, kb_ref, kd_ref, dec_ref,
                   att_ref, upd_ref)
    _merge_and_project(x_ref, gate, ya_ref, z_ref, ga_ref, gb_ref, wa_ref, wb_ref, wo_ref, o_ref)

    @pl.when(jnp.logical_not(factored_ok))
    def _():
        state_ref[...] = state_in_ref[...]
        _hgrn_exact(q_ref, k_ref, v_ref, og_ref, hgrn_gain_ref, ya_ref, state_ref, beta_ref, qd_ref, kb_ref, kd_ref,
                    dec_ref, intra_ref)
        _merge_and_project(x_ref, gate, ya_ref, z_ref, ga_ref, gb_ref, wa_ref, wb_ref, wo_ref, o_ref)


def _mixer(x, mod, gain, w_in, hgrn_lb, hgrn_gain, conv_w, conv_b, ln_g, ln_b, wa, wb, wo):
    B, S, D = x.shape
    tm = MIX_TOKENS
    n_heads = D // HGRN_HEAD_DIM
    n_chunks = tm // HGRN_CHUNK
    tile = pl.BlockSpec((None, tm, D), lambda b, i: (b, i, 0))
    f32_tile = pltpu.VMEM((tm, D), _F32)
    bf16_tile = pltpu.VMEM((tm, D), _BF16)
    state = pltpu.VMEM((n_heads, HGRN_HEAD_DIM, HGRN_HEAD_DIM), _F32)
    dec_rows = -(-3 * n_chunks // SUBLANES) * SUBLANES
    scratch = [f32_tile, f32_tile, f32_tile, bf16_tile, bf16_tile, bf16_tile, bf16_tile,
               bf16_tile,
               pltpu.VMEM((D // LANES, CONV_HALO + tm, LANES), _F32), f32_tile, bf16_tile,
               state, state, f32_tile,
               bf16_tile, bf16_tile, bf16_tile, bf16_tile, bf16_tile,
               pltpu.VMEM((dec_rows, D), _F32), bf16_tile,
               pltpu.VMEM((n_chunks, n_heads, HGRN_HEAD_DIM, HGRN_HEAD_DIM), _F32), f32_tile]
    weights = (w_in.size + wa.size + wb.size + wo.size) * 2
    vmem = (weights + 4 * tm * D * 4 + 7 * tm * D * 4 + 12 * tm * D * 2 + (CONV_HALO + tm) * D * 4
            + (2 + n_chunks) * n_heads * HGRN_HEAD_DIM ** 2 * 4 + 8 * tm * D * 4)
    return pl.pallas_call(
        _mixer_kernel,
        out_shape=jax.ShapeDtypeStruct((B, S, D), _F32),
        grid=(B, S // tm),
        in_specs=[tile,
                  pl.BlockSpec((None, N_MOD, D), lambda b, i: (b, 0, 0)),
                  _resident((1, D)), _resident(w_in.shape), _resident(hgrn_lb.shape), _resident((1, D)),
                  _resident(conv_w.shape), _resident((1, D)), _resident((1, D)), _resident((1, D)),
                  _resident(wa.shape), _resident(wb.shape), _resident(wo.shape)],
        out_specs=tile,
        scratch_shapes=scratch,
        compiler_params=pltpu.CompilerParams(dimension_semantics=("arbitrary", "arbitrary"),
                                             vmem_limit_bytes=_vmem_limit(vmem)),
        name="mixer",
    )(x, mod, gain, w_in, hgrn_lb, hgrn_gain, conv_w, conv_b, ln_g, ln_b, wa, wb, wo)


def kernel(x, c, ada_w, ada_b, norm_ffn1, ffn1_w_in, ffn1_w_out, norm_mix, mix_w_in, hgrn_lb, hgrn_g, hgrn_w_o,
           conv_w, conv_b, conv_ln_g, conv_ln_b, conv_w_o, mix_w_out, norm_ffn2, ffn2_w_in, ffn2_w_out, norm_final):
    B, S, D = x.shape
    depth = ada_w.shape[0]
    assert depth == 1, "single-layer block"
    assert S % FFN_TOKENS == 0 and S % MIX_TOKENS == 0 and MIX_TOKENS % HGRN_CHUNK == 0 and HGRN_CHUNK % (2 * SUBLANES) == 0
    assert mix_w_in.shape[-1] == 8 * D and D % HGRN_HEAD_DIM == 0
    assert conv_w.shape[1] == CONV_K and CONV_K - 1 <= CONV_HALO and MIX_TOKENS % CONV_ROWS == 0

    mod = _adaln_mod(c, ada_w[0], ada_b).reshape(B, N_MOD, D)
    fin = norm_final.reshape(1, D)

    x = _ffn(x, mod, norm_ffn1, ffn1_w_in[0].astype(_BF16), ffn1_w_out[0].astype(_BF16), fin,
             mod_row=0, final_norm=False)

    x = _mixer(x, mod, norm_mix, mix_w_in[0].astype(_BF16), hgrn_lb, hgrn_g, conv_w[0], conv_b, conv_ln_g, conv_ln_b,
               hgrn_w_o[0].astype(_BF16), conv_w_o[0].astype(_BF16), mix_w_out[0].astype(_BF16))

    return _ffn(x, mod, norm_ffn2, ffn2_w_in[0].astype(_BF16), ffn2_w_out[0].astype(_BF16), fin,
                mod_row=6, final_norm=True)
```

```python
import functools

import jax
import jax.numpy as jnp
from jax import lax
from jax.experimental import pallas as pl
from jax.experimental.pallas import tpu as pltpu

EPS = 1e-6
N_MOD = 9
HGRN_HEAD_DIM = 128
CONV_K = 31
FFN_RESIDUAL = 0.5

V7X_VMEM_BYTES = 64 * 1024 * 1024
V7X_MXU_COL = 256
SUBLANES = 8

ADALN_COLS = 3072
FFN_TOKENS = 512
MIX_TOKENS = 256
HGRN_CHUNK = 128
LANES = 128
CONV_HALO = 32
CONV_ROWS = 64
FACTORED_DECAY_LIMIT = 80.0

_BF16 = jnp.bfloat16
_F32 = jnp.float32


def _vmem_limit(nbytes):
    return int(min(nbytes * 1.25 + (4 << 20), V7X_VMEM_BYTES - (8 << 20)))


def _resident(shape):
    nd = len(shape)
    return pl.BlockSpec(shape, lambda *_: (0,) * nd, pipeline_mode=pl.Buffered(1))


def _dot(a, b):
    return jnp.dot(a, b, preferred_element_type=_F32)


def _dot_nt(a, b):
    return lax.dot_general(a, b, (((1,), (1,)), ((), ())), preferred_element_type=_F32)


def _dot_tn(a, b):
    return lax.dot_general(a, b, (((0,), (0,)), ((), ())), preferred_element_type=_F32)


def _silu(x):
    return x * jax.nn.sigmoid(x)


def _modulated_rms_norm(x, gain, scale, shift):
    y = x * lax.rsqrt(jnp.mean(x * x, axis=-1, keepdims=True) + EPS)
    return (y * gain) * (1.0 + scale) + shift


def _adaln_kernel(c_ref, w_ref, b_ref, o_ref):
    cs = _silu(c_ref[...]).astype(_BF16)
    o_ref[...] = _dot(cs, w_ref[...].astype(_BF16)) + b_ref[...]


def _adaln_mod(c, ada_w, ada_b):
    B, D = c.shape
    n = ada_w.shape[1]
    cols = ADALN_COLS
    assert n % cols == 0
    vmem = 2 * D * cols * 4 + D * cols * 2 + 4 * B * cols * 4
    return pl.pallas_call(
        _adaln_kernel,
        out_shape=jax.ShapeDtypeStruct((B, n), _F32),
        grid=(n // cols,),
        in_specs=[pl.BlockSpec((B, D), lambda j: (0, 0)),
                  pl.BlockSpec((D, cols), lambda j: (0, j)),
                  pl.BlockSpec((1, cols), lambda j: (0, j))],
        out_specs=pl.BlockSpec((B, cols), lambda j: (0, j)),
        compiler_params=pltpu.CompilerParams(dimension_semantics=("arbitrary",),
                                             vmem_limit_bytes=_vmem_limit(vmem)),
        name="adaln_mod",
    )(c, ada_w, ada_b)


def _ffn_kernel(x_ref, mod_ref, gain_ref, w_in_ref, w_out_ref, fin_ref, o_ref, act_ref, *, mod_row, final_norm):
    x = x_ref[...]
    shift = mod_ref[pl.ds(mod_row, 1), :]
    scale = mod_ref[pl.ds(mod_row + 1, 1), :]
    gate = mod_ref[pl.ds(mod_row + 2, 1), :]
    h = _modulated_rms_norm(x, gain_ref[...], scale, shift).astype(_BF16)
    F = w_out_ref.shape[0]
    fc = V7X_MXU_COL
    for j in range(F // fc):
        a = _dot(h, w_in_ref[:, j * fc:(j + 1) * fc])
        b = _dot(h, w_in_ref[:, F + j * fc:F + (j + 1) * fc])
        act_ref[:, j * fc:(j + 1) * fc] = (_silu(a) * b).astype(_BF16)
    y = x + (FFN_RESIDUAL * gate) * _dot(act_ref[...], w_out_ref[...])
    if final_norm:
        y = y * lax.rsqrt(jnp.mean(y * y, axis=-1, keepdims=True) + EPS) * fin_ref[...]
    o_ref[...] = y


def _ffn(x, mod, gain, w_in, w_out, fin_gain, *, mod_row, final_norm):
    B, S, D = x.shape
    F = w_out.shape[0]
    assert w_in.shape == (D, 2 * F) and F % V7X_MXU_COL == 0
    tm = FFN_TOKENS
    vmem = (w_in.size + w_out.size) * 2 + 4 * tm * D * 4 + tm * F * 2 + 3 * tm * D * 4 + 4 * tm * V7X_MXU_COL * 4
    return pl.pallas_call(
        functools.partial(_ffn_kernel, mod_row=mod_row, final_norm=final_norm),
        out_shape=jax.ShapeDtypeStruct((B, S, D), _F32),
        grid=(B, S // tm),
        in_specs=[pl.BlockSpec((None, tm, D), lambda b, i: (b, i, 0)),
                  pl.BlockSpec((None, N_MOD, D), lambda b, i: (b, 0, 0)),
                  _resident((1, D)),
                  _resident(w_in.shape),
                  _resident(w_out.shape),
                  _resident((1, D))],
        out_specs=pl.BlockSpec((None, tm, D), lambda b, i: (b, i, 0)),
        scratch_shapes=[pltpu.VMEM((tm, F), _BF16)],
        compiler_params=pltpu.CompilerParams(dimension_semantics=("arbitrary", "arbitrary"),
                                             vmem_limit_bytes=_vmem_limit(vmem)),
        name="ffn_final" if final_norm else "ffn",
    )(x, mod, gain, w_in, w_out, fin_gain)


def _project_hgrn(h, w_ref, lb_ref, q_ref, g_ref, k_ref, v_ref, og_ref):
    D = h.shape[-1]

    def proj(i):
        return _dot(h, w_ref[:, i * D:(i + 1) * D])

    lb_all = lb_ref[...]
    e = jnp.exp(lb_all - jnp.max(lb_all, axis=0, keepdims=True))
    lb = e[0:1, :] / jnp.sum(e, axis=0, keepdims=True)

    fg = lb + (1.0 - lb) * jax.nn.sigmoid(proj(1))
    g_ref[...] = jnp.log(fg)
    k_ref[...] = 1.0 - fg
    q_ref[...] = _silu(proj(0)) * (HGRN_HEAD_DIM ** -0.5)
    v_ref[...] = proj(2).astype(_BF16)
    og_ref[...] = _silu(proj(3)).astype(_BF16)


def _project_conv_input(h, w_ref, ubuf_ref):
    D = h.shape[-1]
    u = _dot(h, w_ref[:, 4 * D:5 * D]) * jax.nn.sigmoid(_dot(h, w_ref[:, 5 * D:6 * D]))
    for l in range(D // LANES):
        ubuf_ref[l, CONV_HALO:, :] = u[:, l * LANES:(l + 1) * LANES]


def _project_gates(h, w_ref, ga_ref, gb_ref):
    D = h.shape[-1]
    ga_ref[...] = jax.nn.sigmoid(_dot(h, w_ref[:, 6 * D:7 * D])).astype(_BF16)
    gb_ref[...] = jax.nn.sigmoid(_dot(h, w_ref[:, 7 * D:8 * D])).astype(_BF16)


def _hgrn_prepare(q_ref, g_ref, k_ref, beta_ref, qd_ref, qs_ref, ki_ref, kb_ref, kd_ref, dec_ref):
    ts = q_ref.shape[0]
    C = HGRN_CHUNK
    half = C // 2
    n_chunks = ts // C
    row = lax.broadcasted_iota(jnp.int32, (C, C), 0)
    col = lax.broadcasted_iota(jnp.int32, (C, C), 1)
    tri = jnp.where((row >= col) & ((row >= half) == (col >= half)), 1.0, 0.0).astype(_BF16)
    first_half = lax.broadcasted_iota(jnp.int32, (C, 1), 0) < half
    worst = None
    for c in range(n_chunks):
        rows = slice(c * C, (c + 1) * C)
        g = g_ref[rows, :]
        g_hi = g.astype(_BF16)
        g_lo = (g - g_hi.astype(_F32)).astype(_BF16)
        beta = _dot(tri, g_hi) + _dot(tri, g_lo)
        l0 = beta[half - 1:half, :]
        l1 = beta[C - 1:C, :]
        both = jnp.minimum(l0, l1)
        worst = both if worst is None else jnp.minimum(worst, both)
        e0 = jnp.exp(l0)
        e1 = jnp.exp(l1)
        qd = q_ref[rows, :] * jnp.exp(beta)
        k = k_ref[rows, :]
        ki = k * jnp.exp(-beta)
        k_end = k * jnp.exp(jnp.where(first_half, l0, l1) - beta)
        beta_ref[rows, :] = beta
        qd_ref[rows, :] = qd.astype(_BF16)
        qs_ref[rows, :] = jnp.where(first_half, qd, qd * e0).astype(_BF16)
        ki_ref[rows, :] = ki.astype(_BF16)
        kb_ref[rows, :] = jnp.where(first_half, k_end, ki).astype(_BF16)
        kd_ref[rows, :] = jnp.where(first_half, k_end * e1, k_end).astype(_BF16)
        dec_ref[c:c + 1, :] = e0 * e1
        dec_ref[n_chunks + 2 * c:n_chunks + 2 * c + 1, :] = e0
        dec_ref[n_chunks + 2 * c + 1:n_chunks + 2 * c + 2, :] = e1
    return jnp.min(worst) >= -FACTORED_DECAY_LIMIT


def _hgrn_finish(o, lanes, gain_ref, og_ref, o_ref):
    o = o * lax.rsqrt(jnp.mean(o * o, axis=-1, keepdims=True) + EPS)
    o = o * gain_ref[:, lanes] * og_ref[:, lanes].astype(_F32)
    o_ref[:, lanes] = o.astype(o_ref.dtype)


def _hgrn_factored(v_ref, og_ref, gain_ref, o_ref, state_ref, qd_ref, qs_ref, ki_ref, kb_ref, kd_ref, dec_ref,
                   att_ref, upd_ref):
    ts, width = v_ref.shape
    hd = HGRN_HEAD_DIM
    C = HGRN_CHUNK
    half = C // 2
    n_chunks = ts // C
    causal = lax.broadcasted_iota(jnp.int32, (C, C), 0) >= lax.broadcasted_iota(jnp.int32, (C, C), 1)
    for h in range(width // hd):
        lanes = slice(h * hd, (h + 1) * hd)
        for c in range(n_chunks):
            rows = slice(c * C, (c + 1) * C)
            top = _dot_nt(qd_ref[c * C:c * C + half, lanes], ki_ref[rows, lanes])
            bottom = _dot_nt(qd_ref[c * C + half:(c + 1) * C, lanes], kb_ref[rows, lanes])
            att_ref[rows, lanes] = jnp.where(causal, jnp.concatenate([top, bottom], axis=0), 0.0).astype(_BF16)
            upd_ref[c, h] = _dot_tn(v_ref[rows, lanes], kd_ref[rows, lanes])
    for h in range(width // hd):
        lanes = slice(h * hd, (h + 1) * hd)
        state = state_ref[h]
        outs = []
        for c in range(n_chunks):
            rows = slice(c * C, (c + 1) * C)
            outs.append(_dot(att_ref[rows, lanes], v_ref[rows, lanes])
                        + _dot_nt(qs_ref[rows, lanes], state.astype(_BF16)))
            state = state * dec_ref[c:c + 1, lanes] + upd_ref[c, h]
        state_ref[h] = state
        _hgrn_finish(jnp.concatenate(outs, axis=0), lanes, gain_ref, og_ref, o_ref)


def _hgrn_exact(q_ref, k_ref, v_ref, og_ref, gain_ref, o_ref, state_ref, beta_ref, qd_ref, kb_ref, kd_ref, dec_ref,
                intra_ref):
    ts, width = v_ref.shape
    hd = HGRN_HEAD_DIM
    half = HGRN_CHUNK // 2
    n_chunks = ts // HGRN_CHUNK
    s_idx = lax.broadcasted_iota(jnp.int32, (half, 1), 0)

    def row_group(r, carry):
        base = pl.multiple_of(r * SUBLANES, SUBLANES)
        start = pl.multiple_of((base // half) * half, half)
        for h in range(width // hd):
            lanes = slice(h * hd, (h + 1) * hd)
            q8 = q_ref[pl.ds(base, SUBLANES), lanes]
            b8 = beta_ref[pl.ds(base, SUBLANES), lanes]
            k_c = k_ref[pl.ds(start, half), lanes]
            b_c = beta_ref[pl.ds(start, half), lanes]
            v_c = v_ref[pl.ds(start, half), lanes].astype(_F32)
            out_rows = []
            for i in range(SUBLANES):
                decay = jnp.exp(jnp.minimum(b8[i:i + 1, :] - b_c, 0.0))
                w = jnp.where(s_idx <= base - start + i, q8[i:i + 1, :] * k_c * decay, 0.0)
                att = jnp.sum(w, axis=-1, keepdims=True)
                out_rows.append(jnp.sum(att * v_c, axis=0, keepdims=True))
            intra_ref[pl.ds(base, SUBLANES), lanes] = jnp.concatenate(out_rows, axis=0)
        return carry

    lax.fori_loop(0, ts // SUBLANES, row_group, 0)
    for h in range(width // hd):
        lanes = slice(h * hd, (h + 1) * hd)
        state = state_ref[h]
        outs = []
        for s in range(2 * n_chunks):
            rows = slice(s * half, (s + 1) * half)
            k_end_ref = kb_ref if s % 2 == 0 else kd_ref
            outs.append(intra_ref[rows, lanes] + _dot_nt(qd_ref[rows, lanes], state.astype(_BF16)))
            state = state * dec_ref[n_chunks + s:n_chunks + s + 1, lanes] + _dot_tn(v_ref[rows, lanes],
                                                                                   k_end_ref[rows, lanes])
        state_ref[h] = state
        _hgrn_finish(jnp.concatenate(outs, axis=0), lanes, gain_ref, og_ref, o_ref)


def _conv_branch(ubuf_ref, conv_w_ref, conv_b_ref, ln_g_ref, ln_b_ref, conv_ref, z_ref):
    tm = conv_ref.shape[0]
    first = CONV_HALO - (CONV_K - 1)
    for l in range(ubuf_ref.shape[0]):
        lanes = slice(l * LANES, (l + 1) * LANES)
        for r in range(0, tm, CONV_ROWS):
            acc = jnp.broadcast_to(conv_b_ref[:, lanes], (CONV_ROWS, LANES))
            for j in range(CONV_K):
                acc = acc + conv_w_ref[j:j + 1, lanes] * ubuf_ref[l, r + first + j:r + first + j + CONV_ROWS, :]
            conv_ref[r:r + CONV_ROWS, lanes] = acc
        ubuf_ref[l, 0:CONV_HALO, :] = ubuf_ref[l, tm:tm + CONV_HALO, :]

    y = conv_ref[...]
    mu = jnp.mean(y, axis=-1, keepdims=True)
    yc = y - mu
    yn = yc * lax.rsqrt(jnp.mean(yc * yc, axis=-1, keepdims=True) + EPS)
    z_ref[...] = _silu(yn * ln_g_ref[...] + ln_b_ref[...]).astype(_BF16)


def _merge_and_project(x_ref, gate, ya_ref, z_ref, ga_ref, gb_ref, wa_ref, wb_ref, wo_ref, o_ref):
    y_a = _dot(ya_ref[...], wa_ref[...])
    y_b = _dot(z_ref[...], wb_ref[...])
    merged = (ga_ref[...].astype(_F32) * y_a + gb_ref[...].astype(_F32) * y_b).astype(_BF16)
    o_ref[...] = x_ref[...] + gate * _dot(merged, wo_ref[...])


def _mixer_kernel(x_ref, mod_ref, gain_ref, w_in_ref, lb_ref, hgrn_gain_ref,
                  conv_w_ref, conv_b_ref, ln_g_ref, ln_b_ref, wa_ref, wb_ref, wo_ref,
                  o_ref,
                  q_ref, g_ref, k_ref, v_ref, og_ref, ga_ref, gb_ref, ya_ref, ubuf_ref, conv_ref, z_ref,
                  state_ref, state_in_ref, beta_ref, qd_ref, qs_ref, ki_ref, kb_ref, kd_ref, dec_ref,
                  att_ref, upd_ref, intra_ref):
    @pl.when(pl.program_id(1) == 0)
    def _():
        state_ref[...] = jnp.zeros_like(state_ref)
        ubuf_ref[:, 0:CONV_HALO, :] = jnp.zeros((ubuf_ref.shape[0], CONV_HALO, LANES), _F32)

    shift = mod_ref[pl.ds(3, 1), :]
    scale = mod_ref[pl.ds(4, 1), :]
    gate = mod_ref[pl.ds(5, 1), :]
    h = _modulated_rms_norm(x_ref[...], gain_ref[...], scale, shift).astype(_BF16)

    _project_conv_input(h, w_in_ref, ubuf_ref)
    _project_hgrn(h, w_in_ref, lb_ref, q_ref, g_ref, k_ref, v_ref, og_ref)
    _project_gates(h, w_in_ref, ga_ref, gb_ref)
    _conv_branch(ubuf_ref, conv_w_ref, conv_b_ref, ln_g_ref, ln_b_ref, conv_ref, z_ref)
    state_in_ref[...] = state_ref[...]
    factored_ok = _hgrn_prepare(q_ref, g_ref, k_ref, beta_ref, qd_ref, qs_ref, ki_ref, kb_ref, kd_ref, dec_ref)
    _hgrn_factored(v_ref, og_ref, hgrn_gain_ref, ya_ref, state_ref, qd_ref, qs_ref, ki_ref, kb_ref, kd_ref, dec_ref,
                   att_ref, upd_ref)
    _merge_and_project(x_ref, gate, ya_ref, z_ref, ga_ref, gb_ref, wa_ref, wb_ref, wo_ref, o_ref)

    @pl.when(jnp.logical_not(factored_ok))
    def _():
        state_ref[...] = state_in_ref[...]
        _hgrn_exact(q_ref, k_ref, v_ref, og_ref, hgrn_gain_ref, ya_ref, state_ref, beta_ref, qd_ref, kb_ref, kd_ref,
                    dec_ref, intra_ref)
        _merge_and_project(x_ref, gate, ya_ref, z_ref, ga_ref, gb_ref, wa_ref, wb_ref, wo_ref, o_ref)


def _mixer(x, mod, gain, w_in, hgrn_lb, hgrn_gain, conv_w, conv_b, ln_g, ln_b, wa, wb, wo):
    B, S, D = x.shape
    tm = MIX_TOKENS
    n_heads = D // HGRN_HEAD_DIM
    n_chunks = tm // HGRN_CHUNK
    tile = pl.BlockSpec((None, tm, D), lambda b, i: (b, i, 0))
    f32_tile = pltpu.VMEM((tm, D), _F32)
    bf16_tile = pltpu.VMEM((tm, D), _BF16)
    state = pltpu.VMEM((n_heads, HGRN_HEAD_DIM, HGRN_HEAD_DIM), _F32)
    dec_rows = -(-3 * n_chunks // SUBLANES) * SUBLANES
    scratch = [f32_tile, f32_tile, f32_tile, bf16_tile, bf16_tile, bf16_tile, bf16_tile,
               bf16_tile,
               pltpu.VMEM((D // LANES, CONV_HALO + tm, LANES), _F32), f32_tile, bf16_tile,
               state, state, f32_tile,
               bf16_tile, bf16_tile, bf16_tile, bf16_tile, bf16_tile,
               pltpu.VMEM((dec_rows, D), _F32), bf16_tile,
               pltpu.VMEM((n_chunks, n_heads, HGRN_HEAD_DIM, HGRN_HEAD_DIM), _F32), f32_tile]
    weights = (w_in.size + wa.size + wb.size + wo.size) * 2
    vmem = (weights + 4 * tm * D * 4 + 7 * tm * D * 4 + 12 * tm * D * 2 + (CONV_HALO + tm) * D * 4
            + (2 + n_chunks) * n_heads * HGRN_HEAD_DIM ** 2 * 4 + 8 * tm * D * 4)
    return pl.pallas_call(
        _mixer_kernel,
        out_shape=jax.ShapeDtypeStruct((B, S, D), _F32),
        grid=(B, S // tm),
        in_specs=[tile,
                  pl.BlockSpec((None, N_MOD, D), lambda b, i: (b, 0, 0)),
                  _resident((1, D)), _resident(w_in.shape), _resident(hgrn_lb.shape), _resident((1, D)),
                  _resident(conv_w.shape), _resident((1, D)), _resident((1, D)), _resident((1, D)),
                  _resident(wa.shape), _resident(wb.shape), _resident(wo.shape)],
        out_specs=tile,
        scratch_shapes=scratch,
        compiler_params=pltpu.CompilerParams(dimension_semantics=("arbitrary", "arbitrary"),
                                             vmem_limit_bytes=_vmem_limit(vmem)),
        name="mixer",
    )(x, mod, gain, w_in, hgrn_lb, hgrn_gain, conv_w, conv_b, ln_g, ln_b, wa, wb, wo)


def kernel(x, c, ada_w, ada_b, norm_ffn1, ffn1_w_in, ffn1_w_out, norm_mix, mix_w_in, hgrn_lb, hgrn_g, hgrn_w_o,
           conv_w, conv_b, conv_ln_g, conv_ln_b, conv_w_o, mix_w_out, norm_ffn2, ffn2_w_in, ffn2_w_out, norm_final):
    B, S, D = x.shape
    depth = ada_w.shape[0]
    assert depth == 1, "single-layer block"
    assert S % FFN_TOKENS == 0 and S % MIX_TOKENS == 0 and MIX_TOKENS % HGRN_CHUNK == 0 and HGRN_CHUNK % (2 * SUBLANES) == 0
    assert mix_w_in.shape[-1] == 8 * D and D % HGRN_HEAD_DIM == 0
    assert conv_w.shape[1] == CONV_K and CONV_K - 1 <= CONV_HALO and MIX_TOKENS % CONV_ROWS == 0

    mod = _adaln_mod(c, ada_w[0], ada_b).reshape(B, N_MOD, D)
    fin = norm_final.reshape(1, D)

    x = _ffn(x, mod, norm_ffn1, ffn1_w_in[0].astype(_BF16), ffn1_w_out[0].astype(_BF16), fin,
             mod_row=0, final_norm=False)

    x = _mixer(x, mod, norm_mix, mix_w_in[0].astype(_BF16), hgrn_lb, hgrn_g, conv_w[0], conv_b, conv_ln_g, conv_ln_b,
               hgrn_w_o[0].astype(_BF16), conv_w_o[0].astype(_BF16), mix_w_out[0].astype(_BF16))

    return _ffn(x, mod, norm_ffn2, ffn2_w_in[0].astype(_BF16), ffn2_w_out[0].astype(_BF16), fin,
                mod_row=6, final_norm=True)
```
